```python
import math
import jax, jax.numpy as jnp
from jax import lax
import numpy as np

D_MODEL = 1024
BATCH = 4
SEQ = 4096
DEPTH = 1
DEC_BATCH = 128
DEC_SEQ = 4
PAST_LEN = 8192
PAGE_SIZE = 128

N_HEADS = 8
HEAD_DIM = 64
ATTN_W = N_HEADS * 2 * HEAD_DIM
POOL_WINDOWS = (2, 4, 8, 16)
POOL_W = D_MODEL
POOL_GROUP = POOL_W // len(POOL_WINDOWS)
POOL_HIST = max(POOL_WINDOWS) - 1
IN_W = 4 * ATTN_W + 2 * POOL_W + 2 * D_MODEL
Q_BLOCK = 128
ATTN_SCALE = HEAD_DIM ** -0.5
NEG_INF = -1e30
NORM_EPS = 1e-6

kernel_name = 'diff_attn_pool_gated_hybrid_step'


def rms_norm(x, g):
    xf = x.astype(jnp.float32)
    y = xf * lax.rsqrt(jnp.mean(xf * xf, axis=-1, keepdims=True) + NORM_EPS)
    return (y * g.astype(jnp.float32)).astype(x.dtype)


def in_proj(x, g_norm, w_in, q_norm_g, k_norm_g):
    B, T, _ = x.shape
    p = jnp.einsum('btd,de->bte', rms_norm(x, g_norm), w_in)
    idx = [ATTN_W, 2 * ATTN_W, 3 * ATTN_W, 4 * ATTN_W, 4 * ATTN_W + POOL_W,
           4 * ATTN_W + 2 * POOL_W, 4 * ATTN_W + 2 * POOL_W + D_MODEL]
    q, k, v, za, u, zp, ga, gp = jnp.split(p, idx, axis=-1)
    q = rms_norm(q.reshape(B, T, N_HEADS, 2, HEAD_DIM), q_norm_g)
    k = rms_norm(k.reshape(B, T, N_HEADS, 2, HEAD_DIM), k_norm_g)
    v = v.reshape(B, T, N_HEADS, 2 * HEAD_DIM)
    return q, k, v, za, u, zp, ga, gp


def diff_attend(q, k, v, mask, lam):
    s = jnp.einsum('bqhcd,bkhcd->bhcqk', q, k, preferred_element_type=jnp.float32) * ATTN_SCALE
    s = jnp.where(mask, s, NEG_INF)
    p = jax.nn.softmax(s, axis=-1)
    pd = p[:, :, 0] - lam * p[:, :, 1]
    o = jnp.einsum('bhqk,bkhe->bqhe', pd.astype(v.dtype), v, preferred_element_type=jnp.float32)
    return o.astype(v.dtype)


def prompt_attention(q, k, v, lam):
    B, T = q.shape[0], q.shape[1]
    nblk = T // Q_BLOCK
    qb = jnp.moveaxis(q.reshape(B, nblk, Q_BLOCK, N_HEADS, 2, HEAD_DIM), 1, 0)
    kpos = jnp.arange(T)

    def block(args):
        i, qi = args
        qpos = i * Q_BLOCK + jnp.arange(Q_BLOCK)
        mask = qpos[:, None] >= kpos[None, :]
        return diff_attend(qi, k, v, mask, lam)

    o = lax.map(block, (jnp.arange(nblk), qb))
    return jnp.moveaxis(o, 0, 1).reshape(B, T, N_HEADS, 2 * HEAD_DIM)


def sample_attention(q, k, v, cache_k_l, cache_v_l, page_table, lam):
    Tn = q.shape[1]

    def one_seq(args):
        qb, kb, vb, pages = args
        kp = cache_k_l[pages].reshape(-1, N_HEADS, 2, HEAD_DIM).astype(kb.dtype)
        vp = cache_v_l[pages].reshape(-1, N_HEADS, 2 * HEAD_DIM).astype(vb.dtype)
        past = kp.shape[0]
        kk = jnp.concatenate([kp, kb], axis=0)
        vv = jnp.concatenate([vp, vb], axis=0)
        mask = jnp.concatenate([jnp.ones((Tn, past), bool), jnp.tril(jnp.ones((Tn, Tn), bool))], axis=1)
        return diff_attend(qb[None], kk[None], vv[None], mask, lam)[0]

    return lax.map(one_seq, (q, k, v, page_table))


def head_out(o, subln_g, lam_init):
    B, T = o.shape[0], o.shape[1]
    return (rms_norm(o, subln_g) * (1.0 - lam_init)).reshape(B, T, ATTN_W)


def pool_mix(u_ext, start_pos, w_grp, pool_scale):
    B, L, C = u_ext.shape
    T = L - POOL_HIST
    c = jnp.cumsum(u_ext.astype(jnp.float32), axis=1)
    c = jnp.concatenate([jnp.zeros((B, 1, C), jnp.float32), c], axis=1)
    hi = c[:, POOL_HIST + 1:]
    u_new = u_ext[:, POOL_HIST:].astype(jnp.float32)
    pos = start_pos + jnp.arange(T)
    outs = []
    for gi, w in enumerate(POOL_WINDOWS):
        sl = slice(gi * POOL_GROUP, (gi + 1) * POOL_GROUP)
        lo = c[:, POOL_HIST + 1 - w: POOL_HIST + 1 - w + T, sl]
        cnt = jnp.minimum(w, pos + 1).astype(jnp.float32)[None, :, None]
        outs.append((hi[..., sl] - lo) / cnt - u_new[..., sl])
    d = jnp.stack(outs, axis=2)
    y = jnp.einsum('btgc,gce->btge', d.astype(w_grp.dtype), w_grp, preferred_element_type=jnp.float32)
    return (y.reshape(B, T, C) * pool_scale.astype(jnp.float32)).astype(u_ext.dtype)


def merge(x, oa, za, op, zp, ga, gp, w_out):
    a = jnp.einsum('btc,cd->btd', oa * jax.nn.silu(za), w_out[:ATTN_W])
    p = jnp.einsum('btc,cd->btd', op * jax.nn.silu(zp), w_out[ATTN_W:])
    return x + jax.nn.sigmoid(ga) * a + jax.nn.sigmoid(gp) * p


def setup_inputs(seed: int = 0) -> dict:
    key = jax.random.key(seed)
    ks = jax.random.split(key, 20)
    n_pages = PAST_LEN // PAGE_SIZE
    n_used = DEC_BATCH * n_pages
    n_phys = n_used + n_used // 4
    f32 = jnp.float32
    x_prompt = jax.random.normal(ks[0], (BATCH, SEQ, D_MODEL), f32)
    x_sample = jax.random.normal(ks[1], (DEC_BATCH, DEC_SEQ, D_MODEL), f32)
    cache_k = jax.random.normal(ks[2], (DEPTH, n_phys, PAGE_SIZE, N_HEADS, 2 * HEAD_DIM), f32)
    cache_v = jax.random.normal(ks[3], (DEPTH, n_phys, PAGE_SIZE, N_HEADS, 2 * HEAD_DIM), f32)
    state_pool = jax.random.normal(ks[4], (DEPTH, DEC_BATCH, POOL_HIST, POOL_W), f32)
    page_table = jax.random.permutation(ks[5], n_phys)[:n_used].reshape(DEC_BATCH, n_pages).astype(jnp.int32)
    g_norm = 1.0 + 0.02 * jax.random.normal(ks[6], (DEPTH, D_MODEL), f32)
    w_in = jax.random.normal(ks[7], (DEPTH, D_MODEL, IN_W), f32) * D_MODEL ** -0.5
    q_norm_g = 1.0 + 0.02 * jax.random.normal(ks[8], (DEPTH, HEAD_DIM), f32)
    k_norm_g = 1.0 + 0.02 * jax.random.normal(ks[9], (DEPTH, HEAD_DIM), f32)
    lambda_params = 0.1 * jax.random.normal(ks[10], (DEPTH, 4, HEAD_DIM), f32)
    subln_g = 1.0 + 0.02 * jax.random.normal(ks[11], (DEPTH, 2 * HEAD_DIM), f32)
    w_grp = jax.random.normal(ks[12], (DEPTH, len(POOL_WINDOWS), POOL_GROUP, POOL_GROUP), f32) * POOL_GROUP ** -0.5
    pool_scale = 1.0 + 0.1 * jax.random.normal(ks[13], (DEPTH, POOL_W), f32)
    w_out = jax.random.normal(ks[14], (DEPTH, ATTN_W + POOL_W, D_MODEL), f32) * (ATTN_W + POOL_W) ** -0.5
    return {'x_prompt': x_prompt, 'x_sample': x_sample, 'cache_k': cache_k, 'cache_v': cache_v,
            'state_pool': state_pool, 'page_table': page_table, 'g_norm': g_norm, 'w_in': w_in,
            'q_norm_g': q_norm_g, 'k_norm_g': k_norm_g, 'lambda_params': lambda_params,
            'subln_g': subln_g, 'w_grp': w_grp, 'pool_scale': pool_scale, 'w_out': w_out}


def reference(x_prompt, x_sample, cache_k, cache_v, state_pool, page_table, g_norm, w_in,
              q_norm_g, k_norm_g, lambda_params, subln_g, w_grp, pool_scale, w_out):
    yp, ys = x_prompt, x_sample
    kp_l, vp_l, pp_l, ks_l, vs_l, ps_l = [], [], [], [], [], []
    for l in range(DEPTH):
        lam_init = 0.8 - 0.6 * math.exp(-0.3 * l)
        lp = lambda_params[l].astype(jnp.float32)
        lam = jnp.exp(jnp.sum(lp[0] * lp[1])) - jnp.exp(jnp.sum(lp[2] * lp[3])) + lam_init

        B, T, _ = yp.shape
        q, k, v, za, u, zp, ga, gp = in_proj(yp, g_norm[l], w_in[l], q_norm_g[l], k_norm_g[l])
        oa = head_out(prompt_attention(q, k, v, lam), subln_g[l], lam_init)
        u_ext = jnp.concatenate([jnp.zeros((B, POOL_HIST, POOL_W), u.dtype), u], axis=1)
        op = pool_mix(u_ext, 0, w_grp[l], pool_scale[l])
        yp_next = merge(yp, oa, za, op, zp, ga, gp, w_out[l])
        kp_l.append(k.reshape(B, T, N_HEADS, 2 * HEAD_DIM))
        vp_l.append(v)
        pp_l.append(u_ext[:, -POOL_HIST:])

        Bs, Ts, _ = ys.shape
        q, k, v, za, u, zp, ga, gp = in_proj(ys, g_norm[l], w_in[l], q_norm_g[l], k_norm_g[l])
        oa = head_out(sample_attention(q, k, v, cache_k[l], cache_v[l], page_table, lam), subln_g[l], lam_init)
        u_ext = jnp.concatenate([state_pool[l].astype(u.dtype), u], axis=1)
        op = pool_mix(u_ext, PAST_LEN, w_grp[l], pool_scale[l])
        ys_next = merge(ys, oa, za, op, zp, ga, gp, w_out[l])
        ks_l.append(k.reshape(Bs, Ts, N_HEADS, 2 * HEAD_DIM))
        vs_l.append(v)
        ps_l.append(u_ext[:, -POOL_HIST:])

        yp, ys = yp_next, ys_next

    k_prompt = jnp.stack(kp_l)
    v_prompt = jnp.stack(vp_l)
    pool_prompt = jnp.stack(pp_l)
    k_sample = jnp.stack(ks_l)
    v_sample = jnp.stack(vs_l)
    pool_sample = jnp.stack(ps_l)
    return (yp, ys, k_prompt, v_prompt, pool_prompt, k_sample, v_sample, pool_sample)
```

```python
import functools
import math

import jax
import jax.numpy as jnp
from jax import lax
from jax.experimental import pallas as pl
from jax.experimental.pallas import tpu as pltpu

D_MODEL = 1024
N_HEADS = 8
HEAD_DIM = 64
HEAD_W = 2 * HEAD_DIM
ATTN_W = N_HEADS * HEAD_W
POOL_WINDOWS = (2, 4, 8, 16)
POOL_W = D_MODEL
POOL_GROUP = POOL_W // len(POOL_WINDOWS)
POOL_HIST = max(POOL_WINDOWS) - 1
HALO = POOL_HIST + 1
PAGE_SIZE = 128
ATTN_SCALE = HEAD_DIM ** -0.5
NEG_INF = -1e30
NORM_EPS = 1e-6
N_MAPS = 2
ROWS_PER_HEAD = 8

F32 = jnp.float32
BF16 = jnp.bfloat16
VMEM_LIMIT = 56 * 1024 * 1024


def _params(sem):
    return pltpu.CompilerParams(dimension_semantics=sem, vmem_limit_bytes=VMEM_LIMIT)


def _proj_body(x_ref, g_ref, w_ref, *refs, norm, scale, out_f32, out_bf16):
    if norm:
        gain_ref, gmat_ref = refs[0], refs[1]
        outs = refs[2:]
    else:
        outs = refs
    x = x_ref[...]
    ms = jnp.mean(x * x, axis=-1, keepdims=True)
    xn = (x * lax.rsqrt(ms + NORM_EPS) * g_ref[...]).astype(BF16)
    p = jnp.dot(xn, w_ref[...], preferred_element_type=F32)
    if norm:
        gm = jnp.dot((p * p).astype(BF16), gmat_ref[...], preferred_element_type=F32)
        p = p * lax.rsqrt(gm + NORM_EPS) * gain_ref[...]
    if scale != 1.0:
        p = p * scale
    o = 0
    if out_f32:
        outs[o][...] = p
        o += 1
    if out_bf16:
        outs[o][...] = p.astype(BF16)


def _proj(x2d, g_norm, w_bf16, col0, ncols, tm, *, gain=None, gmat=None, scale=1.0,
          out_f32=True, out_bf16=False):
    m = x2d.shape[0]
    norm = gain is not None
    in_specs = [
        pl.BlockSpec((tm, D_MODEL), lambda j, i: (i, 0)),
        pl.BlockSpec((1, D_MODEL), lambda j, i: (0, 0)),
        pl.BlockSpec((D_MODEL, D_MODEL), lambda j, i: (0, col0 + j)),
    ]
    args = [x2d, g_norm, w_bf16]
    if norm:
        in_specs += [pl.BlockSpec((1, D_MODEL), lambda j, i: (0, 0)),
                     pl.BlockSpec((D_MODEL, D_MODEL), lambda j, i: (0, 0))]
        args += [gain, gmat]
    out_shape, out_specs = [], []
    for flag, dt in ((out_f32, F32), (out_bf16, BF16)):
        if flag:
            out_shape.append(jax.ShapeDtypeStruct((ncols, m, D_MODEL), dt))
            out_specs.append(pl.BlockSpec((None, tm, D_MODEL), lambda j, i: (j, i, 0)))
    body = functools.partial(_proj_body, norm=norm, scale=scale, out_f32=out_f32, out_bf16=out_bf16)
    return pl.pallas_call(
        body, grid=(ncols, m // tm), in_specs=in_specs, out_specs=out_specs, out_shape=out_shape,
        compiler_params=_params(("arbitrary", "arbitrary")),
    )(*args)


def _lam(lp_ref, lam_init):
    lp = lp_ref[...]
    a = jnp.sum(lp[0:1] * lp[1:2], axis=-1, keepdims=True)
    b = jnp.sum(lp[2:3] * lp[3:4], axis=-1, keepdims=True)
    return jnp.exp(a) - jnp.exp(b) + lam_init


def _head_out(o, sg_ref, lam_init):
    ms = jnp.mean(o * o, axis=-1, keepdims=True)
    return o * lax.rsqrt(ms + NORM_EPS) * sg_ref[...] * (1.0 - lam_init)


def _online_update(s, m, l):
    m_new = jnp.maximum(m, jnp.max(s, axis=-1, keepdims=True))
    alpha = jnp.exp(m - m_new)
    p = jnp.exp(s - m_new)
    l_new = alpha * l + jnp.sum(p, axis=-1, keepdims=True)
    return m_new, l_new, alpha, p


def _prompt_attn_body(q_ref, k_ref, v_ref, lp_ref, sg_ref, o_ref, *, tq, lam_init):
    i = pl.program_id(2)
    q = q_ref[...].astype(F32)
    lane = lax.broadcasted_iota(jnp.int32, q.shape, 1)
    qq = jnp.concatenate([jnp.where(lane < HEAD_DIM, q, 0.0),
                          jnp.where(lane >= HEAD_DIM, q, 0.0)], axis=0).astype(BF16)

    def step(j, carry, masked):
        m, l, acc = carry
        start = pl.multiple_of(j * tq, tq)
        kj = k_ref[pl.ds(start, tq), :]
        vj = v_ref[pl.ds(start, tq), :]
        s = lax.dot_general(qq, kj, (((1,), (1,)), ((), ())), preferred_element_type=F32)
        if masked:
            r = lax.broadcasted_iota(jnp.int32, s.shape, 0)
            r = jnp.where(r >= tq, r - tq, r)
            c = lax.broadcasted_iota(jnp.int32, s.shape, 1)
            s = jnp.where(r >= c, s, NEG_INF)
        m, l, alpha, p = _online_update(s, m, l)
        acc = alpha * acc + jnp.dot(p.astype(BF16), vj, preferred_element_type=F32)
        return m, l, acc

    init = (jnp.full((N_MAPS * tq, 1), NEG_INF, F32), jnp.zeros((N_MAPS * tq, 1), F32),
            jnp.zeros((N_MAPS * tq, HEAD_W), F32))
    carry = lax.fori_loop(0, i, functools.partial(step, masked=False), init)
    m, l, acc = step(i, carry, True)
    o = acc / l
    o = o[:tq] - _lam(lp_ref, lam_init) * o[tq:]
    o_ref[...] = _head_out(o, sg_ref, lam_init)


def _prompt_attn(q, k, v, lp, sg, lam_init, tq):
    b, t, _ = q.shape
    body = functools.partial(_prompt_attn_body, tq=tq, lam_init=lam_init)
    return pl.pallas_call(
        body, grid=(b, N_HEADS, t // tq),
        in_specs=[
            pl.BlockSpec((None, tq, HEAD_W), lambda bi, h, i: (bi, i, h)),
            pl.BlockSpec((None, t, HEAD_W), lambda bi, h, i: (bi, 0, h)),
            pl.BlockSpec((None, t, HEAD_W), lambda bi, h, i: (bi, 0, h)),
            pl.BlockSpec((4, HEAD_DIM), lambda bi, h, i: (0, 0)),
            pl.BlockSpec((1, HEAD_W), lambda bi, h, i: (0, 0)),
        ],
        out_specs=pl.BlockSpec((None, tq, HEAD_W), lambda bi, h, i: (bi, i, h)),
        out_shape=jax.ShapeDtypeStruct((b, t, ATTN_W), F32),
        compiler_params=_params(("arbitrary", "arbitrary", "arbitrary")),
    )(q, k, v, lp, sg)


def _decode_attn_body(pt_ref, q_ref, *refs, n_chunk, lam_init):
    del pt_ref
    k_refs = refs[:n_chunk]
    v_refs = refs[n_chunk:2 * n_chunk]
    kn_ref, vn_ref, lp_ref, sg_ref, o_ref, qbd_ref, m_ref, l_ref, acc_ref = refs[2 * n_chunk:]
    j = pl.program_id(1)
    n_rows = N_HEADS * ROWS_PER_HEAD

    @pl.when(j == 0)
    def _():
        q8 = q_ref[...]
        qt = jnp.concatenate([q8] * N_HEADS, axis=0)
        row = lax.broadcasted_iota(jnp.int32, qt.shape, 0)
        col = lax.broadcasted_iota(jnp.int32, qt.shape, 1)
        keep = (col // HEAD_DIM) == (row // (ROWS_PER_HEAD // N_MAPS))
        qbd_ref[...] = jnp.where(keep, qt, 0.0).astype(BF16)
        m_ref[...] = jnp.full(m_ref.shape, NEG_INF, F32)
        l_ref[...] = jnp.zeros(l_ref.shape, F32)
        acc_ref[...] = jnp.zeros(acc_ref.shape, F32)

    qbd = qbd_ref[...]

    def scores(kmat):
        return lax.dot_general(qbd, kmat.astype(BF16), (((1,), (1,)), ((), ())),
                               preferred_element_type=F32)

    def accumulate(s, v_list):
        m, l, alpha, p = _online_update(s, m_ref[...], l_ref[...])
        pv = None
        for c, vmat in enumerate(v_list):
            pc = p[:, c * PAGE_SIZE:(c + 1) * PAGE_SIZE].astype(BF16)
            d = jnp.dot(pc, vmat.astype(BF16), preferred_element_type=F32)
            pv = d if pv is None else pv + d
        m_ref[...] = m
        l_ref[...] = l
        acc_ref[...] = alpha * acc_ref[...] + pv

    s = jnp.concatenate([scores(k_refs[c][...]) for c in range(n_chunk)], axis=1)
    accumulate(s, [v_refs[c][...] for c in range(n_chunk)])

    @pl.when(j == pl.num_programs(1) - 1)
    def _():
        pad = jnp.zeros((PAGE_SIZE - kn_ref.shape[0], ATTN_W), F32)
        kn = jnp.concatenate([kn_ref[...], pad], axis=0)
        vn = jnp.concatenate([vn_ref[...], pad], axis=0)
        sn = scores(kn)
        row = lax.broadcasted_iota(jnp.int32, sn.shape, 0)
        col = lax.broadcasted_iota(jnp.int32, sn.shape, 1)
        tok = row % (ROWS_PER_HEAD // N_MAPS)
        sn = jnp.where(col <= tok, sn, NEG_INF)
        accumulate(sn, [vn])
        o = acc_ref[...] / l_ref[...]
        row = lax.broadcasted_iota(jnp.int32, o.shape, 0)
        col = lax.broadcasted_iota(jnp.int32, o.shape, 1)
        o = jnp.where((col // HEAD_W) == (row // ROWS_PER_HEAD), o, 0.0)
        o = jnp.sum(o.reshape(N_HEADS, ROWS_PER_HEAD, ATTN_W), axis=0)
        n_tok = ROWS_PER_HEAD // N_MAPS
        o = o[:n_tok] - _lam(lp_ref, lam_init) * o[n_tok:]
        for h in range(N_HEADS):
            cols = slice(h * HEAD_W, (h + 1) * HEAD_W)
            o_ref[:, cols] = _head_out(o[:, cols], sg_ref, lam_init)


def _decode_attn(page_table, q8, cache_k, cache_v, kn8, vn8, lp, sg, lam_init, n_chunk):
    nb, n_pages = page_table.shape
    n_tok = ROWS_PER_HEAD // N_MAPS

    def page_spec(c):
        return pl.BlockSpec((None, PAGE_SIZE, ATTN_W),
                            lambda b, j, pt: (pt[b, j * n_chunk + c], 0, 0))

    in_specs = [pl.BlockSpec((None, ROWS_PER_HEAD, ATTN_W), lambda b, j, pt: (b, 0, 0))]
    in_specs += [page_spec(c) for c in range(n_chunk)]
    in_specs += [page_spec(c) for c in range(n_chunk)]
    in_specs += [
        pl.BlockSpec((None, ROWS_PER_HEAD, ATTN_W), lambda b, j, pt: (b, 0, 0)),
        pl.BlockSpec((None, ROWS_PER_HEAD, ATTN_W), lambda b, j, pt: (b, 0, 0)),
        pl.BlockSpec((4, HEAD_DIM), lambda b, j, pt: (0, 0)),
        pl.BlockSpec((1, HEAD_W), lambda b, j, pt: (0, 0)),
    ]
    n_rows = N_HEADS * ROWS_PER_HEAD
    grid_spec = pltpu.PrefetchScalarGridSpec(
        num_scalar_prefetch=1, grid=(nb, n_pages // n_chunk), in_specs=in_specs,
        out_specs=pl.BlockSpec((None, n_tok, ATTN_W), lambda b, j, pt: (b, 0, 0)),
        scratch_shapes=[pltpu.VMEM((n_rows, ATTN_W), BF16), pltpu.VMEM((n_rows, 1), F32),
                        pltpu.VMEM((n_rows, 1), F32), pltpu.VMEM((n_rows, ATTN_W), F32)],
    )
    body = functools.partial(_decode_attn_body, n_chunk=n_chunk, lam_init=lam_init)
    return pl.pallas_call(
        body, grid_spec=grid_spec, out_shape=jax.ShapeDtypeStruct((nb, n_tok, ATTN_W), F32),
        compiler_params=_params(("arbitrary", "arbitrary")),
    )(page_table, q8, *([cache_k] * n_chunk), *([cache_v] * n_chunk), kn8, vn8, lp, sg)


def _pool_mix(u, window_sum, cnt_of, wg_ref, ps_ref):
    outs = []
    for g, w in enumerate(POOL_WINDOWS):
        cols = slice(g * POOL_GROUP, (g + 1) * POOL_GROUP)
        d = window_sum(w, cols) / cnt_of(w) - u[:, cols]
        y = jnp.dot(d.astype(BF16), wg_ref[g], preferred_element_type=F32)
        outs.append(y * ps_ref[:, cols])
    return jnp.concatenate(outs, axis=1)


def _merge_tail(x, oa, za, op, zp, ga, gp, wo_ref):
    ha = (oa * jax.nn.silu(za)).astype(BF16)
    hp = (op * jax.nn.silu(zp)).astype(BF16)
    a = jnp.dot(ha, wo_ref[0:ATTN_W, :], preferred_element_type=F32)
    p = jnp.dot(hp, wo_ref[ATTN_W:ATTN_W + POOL_W, :], preferred_element_type=F32)
    return x + jax.nn.sigmoid(ga) * a + jax.nn.sigmoid(gp) * p


def _merge_prompt_body(x_ref, oa_ref, za_ref, u_ref, halo_ref, zp_ref, ga_ref, gp_ref,
                       wg_ref, ps_ref, wo_ref, y_ref, ext_ref, *, tm):
    i = pl.program_id(1)
    u = u_ref[...]
    ext_ref[0:HALO, :] = jnp.where(i == 0, 0.0, halo_ref[...])
    ext_ref[HALO:HALO + tm, :] = u
    pos = i * tm + lax.broadcasted_iota(jnp.int32, (tm, 1), 0)

    def window_sum(w, cols):
        s = u[:, cols]
        for k in range(1, w):
            s = s + ext_ref[HALO - k:HALO - k + tm, cols]
        return s

    def cnt_of(w):
        return jnp.minimum(w, pos + 1).astype(F32)

    op = _pool_mix(u, window_sum, cnt_of, wg_ref, ps_ref)
    y_ref[...] = _merge_tail(x_ref[...], oa_ref[...], za_ref[...], op, zp_ref[...],
                             ga_ref[...], gp_ref[...], wo_ref)


def _merge_prompt(x, oa, rest, wg, ps, wo, tm):
    b, t, _ = x.shape

    def rest_spec(c):
        return pl.BlockSpec((None, None, tm, D_MODEL), lambda bi, i: (c, bi, i, 0))

    row_spec = pl.BlockSpec((None, tm, D_MODEL), lambda bi, i: (bi, i, 0))
    halo_spec = pl.BlockSpec((None, None, HALO, D_MODEL),
                             lambda bi, i: (1, bi, jnp.maximum(i * (tm // HALO) - 1, 0), 0))
    body = functools.partial(_merge_prompt_body, tm=tm)
    return pl.pallas_call(
        body, grid=(b, t // tm),
        in_specs=[row_spec, row_spec, rest_spec(0), rest_spec(1), halo_spec, rest_spec(2),
                  rest_spec(3), rest_spec(4),
                  pl.BlockSpec((len(POOL_WINDOWS), POOL_GROUP, POOL_GROUP), lambda bi, i: (0, 0, 0)),
                  pl.BlockSpec((1, POOL_W), lambda bi, i: (0, 0)),
                  pl.BlockSpec((ATTN_W + POOL_W, D_MODEL), lambda bi, i: (0, 0))],
        out_specs=row_spec,
        out_shape=jax.ShapeDtypeStruct((b, t, D_MODEL), F32),
        scratch_shapes=[pltpu.VMEM((HALO + tm, D_MODEL), F32)],
        compiler_params=_params(("arbitrary", "arbitrary")),
    )(x, oa, rest, rest, rest, rest, rest, rest, wg, ps, wo)


def _merge_sample_body(x_ref, oa_ref, za_ref, zp_ref, ga_ref, gp_ref, ext_ref,
                       wg_ref, ps_ref, wo_ref, y_ref, *, start_pos):
    t = pl.program_id(0)
    u = ext_ref[POOL_HIST + t]

    def window_sum(w, cols):
        s = u[:, cols]
        for k in range(1, w):
            s = s + ext_ref[POOL_HIST + t - k][:, cols]
        return s

    def cnt_of(w):
        return jnp.minimum(w, start_pos + t + 1).astype(F32)

    op = _pool_mix(u, window_sum, cnt_of, wg_ref, ps_ref)
    y_ref[...] = _merge_tail(x_ref[...], oa_ref[...], za_ref[...], op, zp_ref[...],
                             ga_ref[...], gp_ref[...], wo_ref)


def _merge_sample(x, oa, rest, ext, wg, ps, wo, start_pos):
    t, b, _ = x.shape

    def rest_spec(c):
        return pl.BlockSpec((None, None, b, D_MODEL), lambda ti: (c, ti, 0, 0))

    row_spec = pl.BlockSpec((None, b, D_MODEL), lambda ti: (ti, 0, 0))
    body = functools.partial(_merge_sample_body, start_pos=start_pos)
    return pl.pallas_call(
        body, grid=(t,),
        in_specs=[row_spec, row_spec, rest_spec(0), rest_spec(2), rest_spec(3), rest_spec(4),
                  pl.BlockSpec(ext.shape, lambda ti: (0, 0, 0)),
                  pl.BlockSpec((len(POOL_WINDOWS), POOL_GROUP, POOL_GROUP), lambda ti: (0, 0, 0)),
                  pl.BlockSpec((1, POOL_W), lambda ti: (0, 0)),
                  pl.BlockSpec((ATTN_W + POOL_W, D_MODEL), lambda ti: (0, 0))],
        out_specs=row_spec,
        out_shape=jax.ShapeDtypeStruct((t, b, D_MODEL), F32),
        compiler_params=_params(("arbitrary",)),
    )(x, oa, rest, rest, rest, rest, ext, wg, ps, wo)


def _in_proj(x2d, g_norm, w_bf16, gq, gk, gmat, tm, q_f32):
    q = _proj(x2d, g_norm, w_bf16, 0, 1, tm, gain=gq, gmat=gmat, scale=ATTN_SCALE,
              out_f32=q_f32, out_bf16=not q_f32)[0][0]
    k32, k16 = _proj(x2d, g_norm, w_bf16, 1, 1, tm, gain=gk, gmat=gmat, out_f32=True, out_bf16=True)
    v32, v16 = _proj(x2d, g_norm, w_bf16, 2, 1, tm, out_f32=True, out_bf16=True)
    rest = _proj(x2d, g_norm, w_bf16, 3, 5, tm, out_f32=True)[0]
    return q, k32[0], k16[0], v32[0], v16[0], rest


def kernel(x_prompt, x_sample, cache_k, cache_v, state_pool, page_table, g_norm, w_in, q_norm_g,
           k_norm_g, lambda_params, subln_g, w_grp, pool_scale, w_out):
    depth = g_norm.shape[0]
    assert depth == 1
    l = 0
    lam_init = 0.8 - 0.6 * math.exp(-0.3 * l)
    bp, tp, _ = x_prompt.shape
    bs, ts, _ = x_sample.shape
    past_len = page_table.shape[1] * PAGE_SIZE

    w_bf16 = w_in[l].astype(BF16)
    wo = w_out[l].astype(BF16)
    wg = w_grp[l].astype(BF16)
    gn = g_norm[l].reshape(1, D_MODEL)
    gq = jnp.tile(q_norm_g[l], N_HEADS * N_MAPS).reshape(1, ATTN_W)
    gk = jnp.tile(k_norm_g[l], N_HEADS * N_MAPS).reshape(1, ATTN_W)
    grp = jnp.arange(ATTN_W) // HEAD_DIM
    gmat = jnp.where(grp[:, None] == grp[None, :], 1.0 / HEAD_DIM, 0.0).astype(BF16)
    lp = lambda_params[l]
    sg = subln_g[l].reshape(1, HEAD_W)
    ps = pool_scale[l].reshape(1, POOL_W)

    xp2d = x_prompt.reshape(bp * tp, D_MODEL)
    q, k32, k16, v32, v16, rest = _in_proj(xp2d, gn, w_bf16, gq, gk, gmat, 512, q_f32=False)
    shp = (bp, tp, ATTN_W)
    oa = _prompt_attn(q.reshape(shp), k16.reshape(shp), v16.reshape(shp), lp, sg, lam_init, 256)
    rest = rest.reshape(5, bp, tp, D_MODEL)
    y_prompt = _merge_prompt(x_prompt, oa, rest, wg, ps, wo, 512)
    k_prompt = k32.reshape(1, bp, tp, N_HEADS, HEAD_W)
    v_prompt = v32.reshape(1, bp, tp, N_HEADS, HEAD_W)
    pool_prompt = rest[1][:, tp - POOL_HIST:][None]

    xs_tm = jnp.swapaxes(x_sample, 0, 1)
    qs, ks32, _, vs32, _, rest_s = _in_proj(xs_tm.reshape(ts * bs, D_MODEL), gn, w_bf16, gq, gk, gmat,
                                            ts * bs, q_f32=True)

    def seq_major(a):
        return jnp.swapaxes(a.reshape(ts, bs, ATTN_W), 0, 1)

    qs, ks, vs = seq_major(qs), seq_major(ks32), seq_major(vs32)
    q8 = jnp.concatenate([qs, qs], axis=1)
    zpad = jnp.zeros((bs, ROWS_PER_HEAD - ts, ATTN_W), F32)
    kn8 = jnp.concatenate([ks, zpad], axis=1)
    vn8 = jnp.concatenate([vs, zpad], axis=1)
    n_phys = cache_k.shape[1]
    ck = cache_k.reshape(n_phys, PAGE_SIZE, ATTN_W)
    cv = cache_v.reshape(n_phys, PAGE_SIZE, ATTN_W)
    oa_s = _decode_attn(page_table, q8, ck, cv, kn8, vn8, lp, sg, lam_init, 8)
    rest_s = rest_s.reshape(5, ts, bs, D_MODEL)
    ext = jnp.concatenate([jnp.swapaxes(state_pool[l], 0, 1), rest_s[1]], axis=0)
    y_s = _merge_sample(xs_tm, jnp.swapaxes(oa_s, 0, 1), rest_s, ext, wg, ps, wo, past_len)
    y_sample = jnp.swapaxes(y_s, 0, 1)
    k_sample = ks.reshape(1, bs, ts, N_HEADS, HEAD_W)
    v_sample = vs.reshape(1, bs, ts, N_HEADS, HEAD_W)
    u_s = jnp.swapaxes(rest_s[1], 0, 1)
    pool_sample = jnp.concatenate([state_pool[l][:, ts:], u_s], axis=1)[None]

    return (y_prompt, y_sample, k_prompt, v_prompt, pool_prompt, k_sample, v_sample, pool_sample)
```

```python
import functools
import math

import jax
import jax.numpy as jnp
from jax import lax
from jax.experimental import pallas as pl
from jax.experimental.pallas import tpu as pltpu

D_MODEL = 1024
N_HEADS = 8
HEAD_DIM = 64
HEAD_W = 2 * HEAD_DIM
ATTN_W = N_HEADS * HEAD_W
POOL_WINDOWS = (2, 4, 8, 16)
POOL_W = D_MODEL
POOL_GROUP = POOL_W // len(POOL_WINDOWS)
POOL_HIST = max(POOL_WINDOWS) - 1
HALO = POOL_HIST + 1
PAGE_SIZE = 128
ATTN_SCALE = HEAD_DIM ** -0.5
NEG_INF = -1e30
NORM_EPS = 1e-6
N_MAPS = 2
ROWS_PER_HEAD = 8

F32 = jnp.float32
BF16 = jnp.bfloat16
VMEM_LIMIT = 56 * 1024 * 1024


def _params(sem):
    return pltpu.CompilerParams(dimension_semantics=sem, vmem_limit_bytes=VMEM_LIMIT)


def _proj_body(x_ref, g_ref, w_ref, *refs, norm, scale, out_f32, out_bf16):
    if norm:
        gain_ref, gmat_ref = refs[0], refs[1]
        outs = refs[2:]
    else:
        outs = refs
    x = x_ref[...]
    ms = jnp.mean(x * x, axis=-1, keepdims=True)
    xn = (x * lax.rsqrt(ms + NORM_EPS) * g_ref[...]).astype(BF16)
    p = jnp.dot(xn, w_ref[...], preferred_element_type=F32)
    if norm:
        gm = jnp.dot((p * p).astype(BF16), gmat_ref[...], preferred_element_type=F32)
        p = p * lax.rsqrt(gm + NORM_EPS) * gain_ref[...]
    if scale != 1.0:
        p = p * scale
    o = 0
    if out_f32:
        outs[o][...] = p
        o += 1
    if out_bf16:
        outs[o][...] = p.astype(BF16)


def _proj(x2d, g_norm, w_bf16, col0, ncols, tm, *, gain=None, gmat=None, scale=1.0,
          out_f32=True, out_bf16=False):
    m = x2d.shape[0]
    norm = gain is not None
    in_specs = [
        pl.BlockSpec((tm, D_MODEL), lambda j, i: (i, 0)),
        pl.BlockSpec((1, D_MODEL), lambda j, i: (0, 0)),
        pl.BlockSpec((D_MODEL, D_MODEL), lambda j, i: (0, col0 + j)),
    ]
    args = [x2d, g_norm, w_bf16]
    if norm:
        in_specs += [pl.BlockSpec((1, D_MODEL), lambda j, i: (0, 0)),
                     pl.BlockSpec((D_MODEL, D_MODEL), lambda j, i: (0, 0))]
        args += [gain, gmat]
    out_shape, out_specs = [], []
    for flag, dt in ((out_f32, F32), (out_bf16, BF16)):
        if flag:
            out_shape.append(jax.ShapeDtypeStruct((ncols, m, D_MODEL), dt))
            out_specs.append(pl.BlockSpec((None, tm, D_MODEL), lambda j, i: (j, i, 0)))
    body = functools.partial(_proj_body, norm=norm, scale=scale, out_f32=out_f32, out_bf16=out_bf16)
    return pl.pallas_call(
        body, grid=(ncols, m // tm), in_specs=in_specs, out_specs=out_specs, out_shape=out_shape,
        compiler_params=_params(("arbitrary", "arbitrary")),
    )(*args)


def _lam(lp_ref, lam_init):
    lp = lp_ref[...]
    a = jnp.sum(lp[0:1] * lp[1:2], axis=-1, keepdims=True)
    b = jnp.sum(lp[2:3] * lp[3:4], axis=-1, keepdims=True)
    return jnp.exp(a) - jnp.exp(b) + lam_init


def _prompt_attn_body(qt_ref, k_ref, vt_ref, lp_ref, sg_ref, o_ref, acc_ref, s_ref, p_ref, *, tq, lam_init):
    i = pl.program_id(2)
    qt = qt_ref[...].astype(F32)
    row = lax.broadcasted_iota(jnp.int32, qt.shape, 0)
    wq = [jnp.where(row < HEAD_DIM, qt, 0.0).astype(BF16), jnp.where(row >= HEAD_DIM, qt, 0.0).astype(BF16)]
    acc_ref[...] = jnp.zeros(acc_ref.shape, F32)

    def scores(j):
        start = pl.multiple_of(j * tq, tq)
        kj = k_ref[pl.ds(start, tq), :]
        for c in range(N_MAPS):
            s_ref[c] = jnp.dot(kj, wq[c], preferred_element_type=F32)

    def softmax(m, l, masked):
        out = []
        for c in range(N_MAPS):
            s = s_ref[c]
            if masked:
                key = lax.broadcasted_iota(jnp.int32, s.shape, 0)
                qry = lax.broadcasted_iota(jnp.int32, s.shape, 1)
                s = jnp.where(key <= qry, s, NEG_INF)
            m_new = jnp.maximum(m[c], jnp.max(s, axis=0, keepdims=True))
            alpha = jnp.exp2(m[c] - m_new)
            p = jnp.exp2(s - m_new)
            p_ref[c] = p.astype(BF16)
            out.append((m_new, alpha * l[c] + jnp.sum(p, axis=0, keepdims=True), alpha))
        return [o[0] for o in out], [o[1] for o in out], [o[2] for o in out]

    def pv_update(j, alpha):
        vtj = vt_ref[j]
        for c in range(N_MAPS):
            acc_ref[c] = alpha[c] * acc_ref[c] + jnp.dot(vtj, p_ref[c], preferred_element_type=F32)

    def body(j, carry):
        m, l, alpha_prev = carry
        pv_update(jnp.maximum(j - 1, 0), alpha_prev)
        m, l, alpha = softmax(m, l, False)
        scores(j + 1)
        return m, l, alpha

    scores(0)
    p_ref[...] = jnp.zeros(p_ref.shape, BF16)
    init = ([jnp.full((1, tq), NEG_INF, F32)] * N_MAPS, [jnp.zeros((1, tq), F32)] * N_MAPS,
            [jnp.ones((1, tq), F32)] * N_MAPS)
    m, l, alpha_prev = lax.fori_loop(0, i, body, init)
    pv_update(jnp.maximum(i - 1, 0), alpha_prev)
    m, (l0, l1), alpha = softmax(m, l, True)
    pv_update(i, alpha)
    o = acc_ref[0] / l0 - _lam(lp_ref, lam_init) * (acc_ref[1] / l1)
    ms = jnp.mean(o * o, axis=0, keepdims=True)
    o = o * lax.rsqrt(ms + NORM_EPS) * sg_ref[...] * (1.0 - lam_init)
    o_ref[...] = o.T


def _prompt_attn(qt, k, vt, lp, sg_col, lam_init, tq):
    b, t, _ = k.shape
    body = functools.partial(_prompt_attn_body, tq=tq, lam_init=lam_init)
    return pl.pallas_call(
        body, grid=(b, N_HEADS, t // tq),
        in_specs=[
            pl.BlockSpec((None, HEAD_W, tq), lambda bi, h, i: (bi, h, i)),
            pl.BlockSpec((None, t, HEAD_W), lambda bi, h, i: (bi, 0, h)),
            pl.BlockSpec((None, None, t // tq, HEAD_W, tq), lambda bi, h, i: (bi, h, 0, 0, 0)),
            pl.BlockSpec((4, HEAD_DIM), lambda bi, h, i: (0, 0)),
            pl.BlockSpec((HEAD_W, 1), lambda bi, h, i: (0, 0)),
        ],
        out_specs=pl.BlockSpec((None, tq, HEAD_W), lambda bi, h, i: (bi, i, h)),
        out_shape=jax.ShapeDtypeStruct((b, t, ATTN_W), F32),
        scratch_shapes=[pltpu.VMEM((N_MAPS, HEAD_W, tq), F32), pltpu.VMEM((N_MAPS, tq, tq), F32),
                        pltpu.VMEM((N_MAPS, tq, tq), BF16)],
        compiler_params=_params(("arbitrary", "arbitrary", "arbitrary")),
    )(qt, k, vt, lp, sg_col)


def _decode_attn_body(pt_ref, q_ref, *refs, n_chunk, lam_init):
    del pt_ref
    k_refs = refs[:n_chunk]
    v_refs = refs[n_chunk:2 * n_chunk]
    (kn_ref, vn_ref, bias_ref, biasn_ref, lp_ref, sg_ref, o_ref,
     qm_ref, m_ref, l_ref, acc_ref, s_ref, sn_ref) = refs[2 * n_chunk:]
    j = pl.program_id(1)
    n_tok = ROWS_PER_HEAD // N_MAPS
    page_rows = PAGE_SIZE * N_HEADS

    @pl.when(j == 0)
    def _():
        q8 = q_ref[...]
        row = lax.broadcasted_iota(jnp.int32, (ROWS_PER_HEAD, HEAD_W), 0)
        lane = lax.broadcasted_iota(jnp.int32, (ROWS_PER_HEAD, HEAD_W), 1)
        keep = (lane // HEAD_DIM) == (row // n_tok)
        blocks = [jnp.where(keep, q8[:, h * HEAD_W:(h + 1) * HEAD_W], 0.0) for h in range(N_HEADS)]
        qm_ref[...] = jnp.concatenate(blocks, axis=0).astype(BF16)
        m_ref[...] = jnp.full(m_ref.shape, NEG_INF, F32)
        l_ref[...] = jnp.zeros(l_ref.shape, F32)
        acc_ref[...] = jnp.zeros(acc_ref.shape, F32)

    qm = qm_ref[...]

    def update(state, kmats, vmats, bias, stage_ref):
        m, l, acc = state
        m_new = m
        for c, kmat in enumerate(kmats):
            s = lax.dot_general(qm, kmat.astype(BF16), (((1,), (1,)), ((), ())),
                                preferred_element_type=F32) + bias
            stage_ref[c] = s
            m_new = jnp.maximum(m_new, jnp.max(s, axis=-1, keepdims=True))
        alpha = jnp.exp2(m - m_new)
        l = alpha * l
        acc = alpha * acc
        for c, vmat in enumerate(vmats):
            p = jnp.exp2(stage_ref[c] - m_new)
            l = l + jnp.sum(p, axis=-1, keepdims=True)
            acc = acc + jnp.dot(p.astype(BF16), vmat.astype(BF16), preferred_element_type=F32)
        return m_new, l, acc

    state = update((m_ref[...], l_ref[...], acc_ref[...]),
                   [k_refs[c][...].reshape(page_rows, HEAD_W) for c in range(n_chunk)],
                   [v_refs[c][...].reshape(page_rows, HEAD_W) for c in range(n_chunk)],
                   bias_ref[...], s_ref)
    m_ref[...], l_ref[...], acc_ref[...] = state

    @pl.when(j == pl.num_programs(1) - 1)
    def _():
        pad = jnp.zeros((PAGE_SIZE - kn_ref.shape[0], HEAD_W), F32)
        kn = jnp.concatenate([kn_ref[...], pad], axis=0)
        vn = jnp.concatenate([vn_ref[...], pad], axis=0)
        _, l, acc = update(state, [kn], [vn], biasn_ref[...], sn_ref)
        o = acc / l
        lam = _lam(lp_ref, lam_init)
        for h in range(N_HEADS):
            blk = o[h * ROWS_PER_HEAD:(h + 1) * ROWS_PER_HEAD]
            r = blk[:n_tok] - lam * blk[n_tok:]
            ms = jnp.mean(r * r, axis=-1, keepdims=True)
            o_ref[:, h * HEAD_W:(h + 1) * HEAD_W] = (
                r * lax.rsqrt(ms + NORM_EPS) * sg_ref[...] * (1.0 - lam_init))


def _decode_bias(n_tok):
    n = jnp.arange(N_HEADS * ROWS_PER_HEAD)[:, None]
    col = jnp.arange(PAGE_SIZE * N_HEADS)[None, :]
    same_head = (col % N_HEADS) == (n // ROWS_PER_HEAD)
    bias = jnp.where(same_head, 0.0, NEG_INF).astype(F32)
    coln = jnp.arange(PAGE_SIZE)[None, :]
    ok = (coln < n_tok * N_HEADS) & ((coln % N_HEADS) == (n // ROWS_PER_HEAD)) & \
         ((coln // N_HEADS) <= (n % n_tok))
    return bias, jnp.where(ok, 0.0, NEG_INF).astype(F32)


def _decode_attn(page_table, q8, cache_k, cache_v, kn, vn, lp, sg, lam_init, n_chunk):
    nb, n_pages = page_table.shape
    n_tok = ROWS_PER_HEAD // N_MAPS
    n_rows = N_HEADS * ROWS_PER_HEAD
    bias, biasn = _decode_bias(n_tok)

    def page_spec(c):
        return pl.BlockSpec((None, None, PAGE_SIZE, N_HEADS, HEAD_W),
                            lambda b, j, pt: (0, pt[b, j * n_chunk + c], 0, 0, 0))

    def const_spec(shape):
        return pl.BlockSpec(shape, lambda b, j, pt: (0,) * len(shape))

    new_spec = pl.BlockSpec((None, n_tok * N_HEADS, HEAD_W), lambda b, j, pt: (b, 0, 0))
    in_specs = [pl.BlockSpec((None, ROWS_PER_HEAD, ATTN_W), lambda b, j, pt: (b, 0, 0))]
    in_specs += [page_spec(c) for c in range(n_chunk)]
    in_specs += [page_spec(c) for c in range(n_chunk)]
    in_specs += [new_spec, new_spec, const_spec(bias.shape), const_spec(biasn.shape),
                 const_spec((4, HEAD_DIM)), const_spec((1, HEAD_W))]
    grid_spec = pltpu.PrefetchScalarGridSpec(
        num_scalar_prefetch=1, grid=(nb, n_pages // n_chunk), in_specs=in_specs,
        out_specs=pl.BlockSpec((None, n_tok, ATTN_W), lambda b, j, pt: (b, 0, 0)),
        scratch_shapes=[pltpu.VMEM((n_rows, HEAD_W), BF16), pltpu.VMEM((n_rows, 1), F32),
                        pltpu.VMEM((n_rows, 1), F32), pltpu.VMEM((n_rows, HEAD_W), F32),
                        pltpu.VMEM((n_chunk, n_rows, PAGE_SIZE * N_HEADS), F32),
                        pltpu.VMEM((1, n_rows, PAGE_SIZE), F32)],
    )
    body = functools.partial(_decode_attn_body, n_chunk=n_chunk, lam_init=lam_init)
    return pl.pallas_call(
        body, grid_spec=grid_spec, out_shape=jax.ShapeDtypeStruct((nb, n_tok, ATTN_W), F32),
        compiler_params=_params(("arbitrary", "arbitrary")),
    )(page_table, q8, *([cache_k] * n_chunk), *([cache_v] * n_chunk), kn, vn, bias, biasn, lp, sg)


def _pool_mix(u, window_sum, cnt_of, wg_ref, ps_ref):
    outs = []
    for g, w in enumerate(POOL_WINDOWS):
        cols = slice(g * POOL_GROUP, (g + 1) * POOL_GROUP)
        d = window_sum(w, cols) / cnt_of(w) - u[:, cols]
        y = jnp.dot(d.astype(BF16), wg_ref[g], preferred_element_type=F32)
        outs.append(y * ps_ref[:, cols])
    return jnp.concatenate(outs, axis=1)


def _merge_tail(x, oa, za, op, zp, ga, gp, wo_ref):
    ha = (oa * jax.nn.silu(za)).astype(BF16)
    hp = (op * jax.nn.silu(zp)).astype(BF16)
    a = jnp.dot(ha, wo_ref[0:ATTN_W, :], preferred_element_type=F32)
    p = jnp.dot(hp, wo_ref[ATTN_W:ATTN_W + POOL_W, :], preferred_element_type=F32)
    return x + jax.nn.sigmoid(ga) * a + jax.nn.sigmoid(gp) * p


def _merge_prompt_body(x_ref, oa_ref, za_ref, u_ref, halo_ref, zp_ref, ga_ref, gp_ref,
                       wg_ref, ps_ref, wo_ref, y_ref, ext_ref, *, tm):
    i = pl.program_id(1)
    u = u_ref[...]
    ext_ref[0:HALO, :] = jnp.where(i == 0, 0.0, halo_ref[...])
    ext_ref[HALO:HALO + tm, :] = u
    pos = i * tm + lax.broadcasted_iota(jnp.int32, (tm, 1), 0)

    def window_sum(w, cols):
        s = u[:, cols]
        for k in range(1, w):
            s = s + ext_ref[HALO - k:HALO - k + tm, cols]
        return s

    def cnt_of(w):
        return jnp.minimum(w, pos + 1).astype(F32)

    op = _pool_mix(u, window_sum, cnt_of, wg_ref, ps_ref)
    y_ref[...] = _merge_tail(x_ref[...], oa_ref[...], za_ref[...], op, zp_ref[...],
                             ga_ref[...], gp_ref[...], wo_ref)


def _merge_prompt(x, oa, rest, wg, ps, wo, tm):
    b, t, _ = x.shape

    def rest_spec(c):
        return pl.BlockSpec((None, None, tm, D_MODEL), lambda bi, i: (c, bi, i, 0))

    row_spec = pl.BlockSpec((None, tm, D_MODEL), lambda bi, i: (bi, i, 0))
    halo_spec = pl.BlockSpec((None, None, HALO, D_MODEL),
                             lambda bi, i: (1, bi, jnp.maximum(i * (tm // HALO) - 1, 0), 0))
    body = functools.partial(_merge_prompt_body, tm=tm)
    return pl.pallas_call(
        body, grid=(b, t // tm),
        in_specs=[row_spec, row_spec, rest_spec(0), rest_spec(1), halo_spec, rest_spec(2),
                  rest_spec(3), rest_spec(4),
                  pl.BlockSpec((len(POOL_WINDOWS), POOL_GROUP, POOL_GROUP), lambda bi, i: (0, 0, 0)),
                  pl.BlockSpec((1, POOL_W), lambda bi, i: (0, 0)),
                  pl.BlockSpec((ATTN_W + POOL_W, D_MODEL), lambda bi, i: (0, 0))],
        out_specs=row_spec,
        out_shape=jax.ShapeDtypeStruct((b, t, D_MODEL), F32),
        scratch_shapes=[pltpu.VMEM((HALO + tm, D_MODEL), F32)],
        compiler_params=_params(("arbitrary", "arbitrary")),
    )(x, oa, rest, rest, rest, rest, rest, rest, wg, ps, wo)


def _merge_sample_body(x_ref, oa_ref, za_ref, zp_ref, ga_ref, gp_ref, ext_ref,
                       wg_ref, ps_ref, wo_ref, y_ref, *, start_pos):
    t = pl.program_id(0)
    u = ext_ref[POOL_HIST + t]

    def window_sum(w, cols):
        s = u[:, cols]
        for k in range(1, w):
            s = s + ext_ref[POOL_HIST + t - k][:, cols]
        return s

    def cnt_of(w):
        return jnp.minimum(w, start_pos + t + 1).astype(F32)

    op = _pool_mix(u, window_sum, cnt_of, wg_ref, ps_ref)
    y_ref[...] = _merge_tail(x_ref[...], oa_ref[...], za_ref[...], op, zp_ref[...],
                             ga_ref[...], gp_ref[...], wo_ref)


def _merge_sample(x, oa, rest, ext, wg, ps, wo, start_pos):
    t, b, _ = x.shape

    def rest_spec(c):
        return pl.BlockSpec((None, None, b, D_MODEL), lambda ti: (c, ti, 0, 0))

    row_spec = pl.BlockSpec((None, b, D_MODEL), lambda ti: (ti, 0, 0))
    body = functools.partial(_merge_sample_body, start_pos=start_pos)
    return pl.pallas_call(
        body, grid=(t,),
        in_specs=[row_spec, row_spec, rest_spec(0), rest_spec(2), rest_spec(3), rest_spec(4),
                  pl.BlockSpec(ext.shape, lambda ti: (0, 0, 0)),
                  pl.BlockSpec((len(POOL_WINDOWS), POOL_GROUP, POOL_GROUP), lambda ti: (0, 0, 0)),
                  pl.BlockSpec((1, POOL_W), lambda ti: (0, 0)),
                  pl.BlockSpec((ATTN_W + POOL_W, D_MODEL), lambda ti: (0, 0))],
        out_specs=row_spec,
        out_shape=jax.ShapeDtypeStruct((t, b, D_MODEL), F32),
        compiler_params=_params(("arbitrary",)),
    )(x, oa, rest, rest, rest, rest, ext, wg, ps, wo)


Q_SCALE = ATTN_SCALE * math.log2(math.e)
PROMPT_PROJ_ROWS = 512
PROMPT_ATTN_TILE = 256
PROMPT_MERGE_ROWS = 512
DECODE_PAGES_PER_STEP = 8


def _in_proj(x2d, g_norm, w_bf16, gq, gk, gmat, tm, q_f32):
    q = _proj(x2d, g_norm, w_bf16, 0, 1, tm, gain=gq, gmat=gmat, scale=Q_SCALE,
              out_f32=q_f32, out_bf16=not q_f32)[0][0]
    k32, k16 = _proj(x2d, g_norm, w_bf16, 1, 1, tm, gain=gk, gmat=gmat, out_f32=True, out_bf16=True)
    v32, v16 = _proj(x2d, g_norm, w_bf16, 2, 1, tm, out_f32=True, out_bf16=True)
    rest = _proj(x2d, g_norm, w_bf16, 3, 5, tm, out_f32=True)[0]
    return q, k32[0], k16[0], v32[0], v16[0], rest


def kernel(x_prompt, x_sample, cache_k, cache_v, state_pool, page_table, g_norm, w_in, q_norm_g,
           k_norm_g, lambda_params, subln_g, w_grp, pool_scale, w_out):
    depth = g_norm.shape[0]
    assert depth == 1
    l = 0
    lam_init = 0.8 - 0.6 * math.exp(-0.3 * l)
    bp, tp, _ = x_prompt.shape
    bs, ts, _ = x_sample.shape
    assert ts * N_MAPS == ROWS_PER_HEAD
    past_len = page_table.shape[1] * PAGE_SIZE

    w_bf16 = w_in[l].astype(BF16)
    wo = w_out[l].astype(BF16)
    wg = w_grp[l].astype(BF16)
    gn = g_norm[l].reshape(1, D_MODEL)
    gq = jnp.tile(q_norm_g[l], N_HEADS * N_MAPS).reshape(1, ATTN_W)
    gk = jnp.tile(k_norm_g[l], N_HEADS * N_MAPS).reshape(1, ATTN_W)
    grp = jnp.arange(ATTN_W) // HEAD_DIM
    gmat = jnp.where(grp[:, None] == grp[None, :], 1.0 / HEAD_DIM, 0.0).astype(BF16)
    lp = lambda_params[l]
    sg = subln_g[l].reshape(1, HEAD_W)
    ps = pool_scale[l].reshape(1, POOL_W)

    xp2d = x_prompt.reshape(bp * tp, D_MODEL)
    q, k32, k16, v32, v16, rest = _in_proj(xp2d, gn, w_bf16, gq, gk, gmat, PROMPT_PROJ_ROWS, q_f32=False)
    tq = PROMPT_ATTN_TILE
    qt = jnp.swapaxes(q.reshape(bp, tp, ATTN_W), 1, 2)
    vt = v16.reshape(bp, tp // tq, tq, N_HEADS, HEAD_W).transpose(0, 3, 1, 4, 2)
    oa = _prompt_attn(qt, k16.reshape(bp, tp, ATTN_W), vt, lp, sg.reshape(HEAD_W, 1), lam_init, tq)
    rest = rest.reshape(5, bp, tp, D_MODEL)
    y_prompt = _merge_prompt(x_prompt, oa, rest, wg, ps, wo, PROMPT_MERGE_ROWS)
    k_prompt = k32.reshape(1, bp, tp, N_HEADS, HEAD_W)
    v_prompt = v32.reshape(1, bp, tp, N_HEADS, HEAD_W)
    pool_prompt = rest[1][:, tp - POOL_HIST:][None]

    xs_tm = jnp.swapaxes(x_sample, 0, 1)
    qs, ks32, _, vs32, _, rest_s = _in_proj(xs_tm.reshape(ts * bs, D_MODEL), gn, w_bf16, gq, gk, gmat,
                                            ts * bs, q_f32=True)

    def seq_major(a):
        return jnp.swapaxes(a.reshape(ts, bs, ATTN_W), 0, 1)

    qs, ks, vs = seq_major(qs), seq_major(ks32), seq_major(vs32)
    q8 = jnp.concatenate([qs, qs], axis=1)
    kn = ks.reshape(bs, ts * N_HEADS, HEAD_W)
    vn = vs.reshape(bs, ts * N_HEADS, HEAD_W)
    oa_s = _decode_attn(page_table, q8, cache_k, cache_v, kn, vn, lp, sg, lam_init, DECODE_PAGES_PER_STEP)
    rest_s = rest_s.reshape(5, ts, bs, D_MODEL)
    ext = jnp.concatenate([jnp.swapaxes(state_pool[l], 0, 1), rest_s[1]], axis=0)
    y_s = _merge_sample(xs_tm, jnp.swapaxes(oa_s, 0, 1), rest_s, ext, wg, ps, wo, past_len)
    y_sample = jnp.swapaxes(y_s, 0, 1)
    k_sample = ks.reshape(1, bs, ts, N_HEADS, HEAD_W)
    v_sample = vs.reshape(1, bs, ts, N_HEADS, HEAD_W)
    u_s = jnp.swapaxes(rest_s[1], 0, 1)
    pool_sample = jnp.concatenate([state_pool[l][:, ts:], u_s], axis=1)[None]

    return (y_prompt, y_sample, k_prompt, v_prompt, pool_prompt, k_sample, v_sample, pool_sample)
```

```python
import functools
import math

import jax
import jax.numpy as jnp
from jax import lax
from jax.experimental import pallas as pl
from jax.experimental.pallas import tpu as pltpu

D_MODEL = 1024
N_HEADS = 8
HEAD_DIM = 64
HEAD_W = 2 * HEAD_DIM
ATTN_W = N_HEADS * HEAD_W
POOL_WINDOWS = (2, 4, 8, 16)
POOL_W = D_MODEL
POOL_GROUP = POOL_W // len(POOL_WINDOWS)
POOL_HIST = max(POOL_WINDOWS) - 1
HALO = POOL_HIST + 1
PAGE_SIZE = 128
ATTN_SCALE = HEAD_DIM ** -0.5
NEG_INF = -1e30
NORM_EPS = 1e-6
N_MAPS = 2
ROWS_PER_HEAD = 8

F32 = jnp.float32
BF16 = jnp.bfloat16
VMEM_LIMIT = 56 * 1024 * 1024


def _params(sem):
    return pltpu.CompilerParams(dimension_semantics=sem, vmem_limit_bytes=VMEM_LIMIT)


N_REST = 5


def _in_proj_body(x_ref, g_ref, w_ref, gq_ref, gk_ref, gmat_ref,
                  q_ref, k32_ref, k16_ref, v32_ref, v16_ref, rest_ref, *, q_scale):
    x = x_ref[...]
    ms = jnp.mean(x * x, axis=-1, keepdims=True)
    xn = (x * lax.rsqrt(ms + NORM_EPS) * g_ref[...]).astype(BF16)

    def group(g):
        return jnp.dot(xn, w_ref[:, g * D_MODEL:(g + 1) * D_MODEL], preferred_element_type=F32)

    def head_norm(p, gain_ref):
        gm = jnp.dot((p * p).astype(BF16), gmat_ref[...], preferred_element_type=F32)
        return p * lax.rsqrt(gm + NORM_EPS) * gain_ref[...]

    q_ref[...] = (head_norm(group(0), gq_ref) * q_scale).astype(q_ref.dtype)
    k = head_norm(group(1), gk_ref)
    k32_ref[...] = k
    k16_ref[...] = k.astype(BF16)
    v = group(2)
    v32_ref[...] = v
    v16_ref[...] = v.astype(BF16)
    for r in range(N_REST):
        rest_ref[r] = group(3 + r)


def _in_proj(x2d, g_norm, w_bf16, gq, gk, gmat, tm, q_scale, q_dtype):
    m = x2d.shape[0]
    in_w = w_bf16.shape[1]

    def resident(shape):
        return pl.BlockSpec(shape, lambda i: (0,) * len(shape), pipeline_mode=pl.Buffered(1))

    row = pl.BlockSpec((tm, D_MODEL), lambda i: (i, 0))
    outs = [(q_dtype, row), (F32, row), (BF16, row), (F32, row), (BF16, row)]
    out_shape = [jax.ShapeDtypeStruct((m, D_MODEL), dt) for dt, _ in outs]
    out_shape.append(jax.ShapeDtypeStruct((N_REST, m, D_MODEL), F32))
    out_specs = [spec for _, spec in outs] + [pl.BlockSpec((N_REST, tm, D_MODEL), lambda i: (0, i, 0))]
    return pl.pallas_call(
        functools.partial(_in_proj_body, q_scale=q_scale), grid=(m // tm,),
        in_specs=[row, resident((1, D_MODEL)), resident((D_MODEL, in_w)), resident((1, ATTN_W)),
                  resident((1, ATTN_W)), resident((ATTN_W, ATTN_W))],
        out_specs=out_specs, out_shape=out_shape,
        compiler_params=_params(("arbitrary",)),
    )(x2d, g_norm, w_bf16, gq, gk, gmat)


def _lam(lp_ref, lam_init):
    lp = lp_ref[...]
    a = jnp.sum(lp[0:1] * lp[1:2], axis=-1, keepdims=True)
    b = jnp.sum(lp[2:3] * lp[3:4], axis=-1, keepdims=True)
    return jnp.exp(a) - jnp.exp(b) + lam_init


def _prompt_attn_body(qt_ref, k_ref, vt_ref, lp_ref, sg_ref, o_ref, acc_ref, s_ref, p_ref, *, tq, lam_init):
    i = pl.program_id(2)
    qt = qt_ref[...].astype(F32)
    row = lax.broadcasted_iota(jnp.int32, qt.shape, 0)
    wq = [jnp.where(row < HEAD_DIM, qt, 0.0).astype(BF16), jnp.where(row >= HEAD_DIM, qt, 0.0).astype(BF16)]
    acc_ref[...] = jnp.zeros(acc_ref.shape, F32)

    def scores(j):
        start = pl.multiple_of(j * tq, tq)
        kj = k_ref[pl.ds(start, tq), :]
        for c in range(N_MAPS):
            s_ref[c] = jnp.dot(kj, wq[c], preferred_element_type=F32)

    def softmax(m, l, masked):
        out = []
        for c in range(N_MAPS):
            s = s_ref[c]
            if masked:
                key = lax.broadcasted_iota(jnp.int32, s.shape, 0)
                qry = lax.broadcasted_iota(jnp.int32, s.shape, 1)
                s = jnp.where(key <= qry, s, NEG_INF)
            m_new = jnp.maximum(m[c], jnp.max(s, axis=0, keepdims=True))
            alpha = jnp.exp2(m[c] - m_new)
            p = jnp.exp2(s - m_new)
            p_ref[c] = p.astype(BF16)
            out.append((m_new, alpha * l[c] + jnp.sum(p, axis=0, keepdims=True), alpha))
        return [o[0] for o in out], [o[1] for o in out], [o[2] for o in out]

    def pv_update(j, alpha):
        vtj = vt_ref[j]
        for c in range(N_MAPS):
            acc_ref[c] = alpha[c] * acc_ref[c] + jnp.dot(vtj, p_ref[c], preferred_element_type=F32)

    def body(j, carry):
        m, l, alpha_prev = carry
        pv_update(jnp.maximum(j - 1, 0), alpha_prev)
        m, l, alpha = softmax(m, l, False)
        scores(j + 1)
        return m, l, alpha

    scores(0)
    p_ref[...] = jnp.zeros(p_ref.shape, BF16)
    init = ([jnp.full((1, tq), NEG_INF, F32)] * N_MAPS, [jnp.zeros((1, tq), F32)] * N_MAPS,
            [jnp.ones((1, tq), F32)] * N_MAPS)
    m, l, alpha_prev = lax.fori_loop(0, i, body, init)
    pv_update(jnp.maximum(i - 1, 0), alpha_prev)
    m, (l0, l1), alpha = softmax(m, l, True)
    pv_update(i, alpha)
    o = acc_ref[0] / l0 - _lam(lp_ref, lam_init) * (acc_ref[1] / l1)
    ms = jnp.mean(o * o, axis=0, keepdims=True)
    o = o * lax.rsqrt(ms + NORM_EPS) * sg_ref[...] * (1.0 - lam_init)
    o_ref[...] = o.T


def _prompt_attn(qt, k, vt, lp, sg_col, lam_init, tq):
    b, t, _ = k.shape
    body = functools.partial(_prompt_attn_body, tq=tq, lam_init=lam_init)
    return pl.pallas_call(
        body, grid=(b, N_HEADS, t // tq),
        in_specs=[
            pl.BlockSpec((None, HEAD_W, tq), lambda bi, h, i: (bi, h, i)),
            pl.BlockSpec((None, t, HEAD_W), lambda bi, h, i: (bi, 0, h)),
            pl.BlockSpec((None, None, t // tq, HEAD_W, tq), lambda bi, h, i: (bi, h, 0, 0, 0)),
            pl.BlockSpec((4, HEAD_DIM), lambda bi, h, i: (0, 0)),
            pl.BlockSpec((HEAD_W, 1), lambda bi, h, i: (0, 0)),
        ],
        out_specs=pl.BlockSpec((None, tq, HEAD_W), lambda bi, h, i: (bi, i, h)),
        out_shape=jax.ShapeDtypeStruct((b, t, ATTN_W), F32),
        scratch_shapes=[pltpu.VMEM((N_MAPS, HEAD_W, tq), F32), pltpu.VMEM((N_MAPS, tq, tq), F32),
                        pltpu.VMEM((N_MAPS, tq, tq), BF16)],
        compiler_params=_params(("arbitrary", "arbitrary", "arbitrary")),
    )(qt, k, vt, lp, sg_col)


def _decode_attn_body(pt_ref, q_ref, *refs, n_chunk, lam_init):
    del pt_ref
    k_refs = refs[:n_chunk]
    v_refs = refs[n_chunk:2 * n_chunk]
    (kn_ref, vn_ref, bias_ref, biasn_ref, lp_ref, sg_ref, o_ref,
     qm_ref, m_ref, l_ref, acc_ref, s_ref, sn_ref) = refs[2 * n_chunk:]
    j = pl.program_id(1)
    n_tok = ROWS_PER_HEAD // N_MAPS
    page_rows = PAGE_SIZE * N_HEADS

    @pl.when(j == 0)
    def _():
        q8 = q_ref[...]
        row = lax.broadcasted_iota(jnp.int32, (ROWS_PER_HEAD, HEAD_W), 0)
        lane = lax.broadcasted_iota(jnp.int32, (ROWS_PER_HEAD, HEAD_W), 1)
        keep = (lane // HEAD_DIM) == (row // n_tok)
        blocks = [jnp.where(keep, q8[:, h * HEAD_W:(h + 1) * HEAD_W], 0.0) for h in range(N_HEADS)]
        qm_ref[...] = jnp.concatenate(blocks, axis=0).astype(BF16)
        m_ref[...] = jnp.full(m_ref.shape, NEG_INF, F32)
        l_ref[...] = jnp.zeros(l_ref.shape, F32)
        acc_ref[...] = jnp.zeros(acc_ref.shape, F32)

    qm = qm_ref[...]

    def update(state, kmats, vmats, bias, stage_ref):
        m, l, acc = state
        m_new = m
        for c, kmat in enumerate(kmats):
            s = lax.dot_general(qm, kmat.astype(BF16), (((1,), (1,)), ((), ())),
                                preferred_element_type=F32) + bias
            stage_ref[c] = s
            m_new = jnp.maximum(m_new, jnp.max(s, axis=-1, keepdims=True))
        alpha = jnp.exp2(m - m_new)
        l = alpha * l
        acc = alpha * acc
        for c, vmat in enumerate(vmats):
            p = jnp.exp2(stage_ref[c] - m_new)
            l = l + jnp.sum(p, axis=-1, keepdims=True)
            acc = acc + jnp.dot(p.astype(BF16), vmat.astype(BF16), preferred_element_type=F32)
        return m_new, l, acc

    state = update((m_ref[...], l_ref[...], acc_ref[...]),
                   [k_refs[c][...].reshape(page_rows, HEAD_W) for c in range(n_chunk)],
                   [v_refs[c][...].reshape(page_rows, HEAD_W) for c in range(n_chunk)],
                   bias_ref[...], s_ref)
    m_ref[...], l_ref[...], acc_ref[...] = state

    @pl.when(j == pl.num_programs(1) - 1)
    def _():
        pad = jnp.zeros((PAGE_SIZE - kn_ref.shape[0], HEAD_W), F32)
        kn = jnp.concatenate([kn_ref[...], pad], axis=0)
        vn = jnp.concatenate([vn_ref[...], pad], axis=0)
        _, l, acc = update(state, [kn], [vn], biasn_ref[...], sn_ref)
        o = acc / l
        lam = _lam(lp_ref, lam_init)
        for h in range(N_HEADS):
            blk = o[h * ROWS_PER_HEAD:(h + 1) * ROWS_PER_HEAD]
            r = blk[:n_tok] - lam * blk[n_tok:]
            ms = jnp.mean(r * r, axis=-1, keepdims=True)
            o_ref[:, h * HEAD_W:(h + 1) * HEAD_W] = (
                r * lax.rsqrt(ms + NORM_EPS) * sg_ref[...] * (1.0 - lam_init))


def _decode_bias(n_tok):
    n = jnp.arange(N_HEADS * ROWS_PER_HEAD)[:, None]
    col = jnp.arange(PAGE_SIZE * N_HEADS)[None, :]
    same_head = (col % N_HEADS) == (n // ROWS_PER_HEAD)
    bias = jnp.where(same_head, 0.0, NEG_INF).astype(F32)
    coln = jnp.arange(PAGE_SIZE)[None, :]
    ok = (coln < n_tok * N_HEADS) & ((coln % N_HEADS) == (n // ROWS_PER_HEAD)) & \
         ((coln // N_HEADS) <= (n % n_tok))
    return bias, jnp.where(ok, 0.0, NEG_INF).astype(F32)


def _decode_attn(page_table, q8, cache_k, cache_v, kn, vn, lp, sg, lam_init, n_chunk):
    nb, n_pages = page_table.shape
    n_tok = ROWS_PER_HEAD // N_MAPS
    n_rows = N_HEADS * ROWS_PER_HEAD
    bias, biasn = _decode_bias(n_tok)

    def page_spec(c):
        return pl.BlockSpec((None, None, PAGE_SIZE, N_HEADS, HEAD_W),
                            lambda b, j, pt: (0, pt[b, j * n_chunk + c], 0, 0, 0))

    def const_spec(shape):
        return pl.BlockSpec(shape, lambda b, j, pt: (0,) * len(shape))

    new_spec = pl.BlockSpec((None, n_tok * N_HEADS, HEAD_W), lambda b, j, pt: (b, 0, 0))
    in_specs = [pl.BlockSpec((None, ROWS_PER_HEAD, ATTN_W), lambda b, j, pt: (b, 0, 0))]
    in_specs += [page_spec(c) for c in range(n_chunk)]
    in_specs += [page_spec(c) for c in range(n_chunk)]
    in_specs += [new_spec, new_spec, const_spec(bias.shape), const_spec(biasn.shape),
                 const_spec((4, HEAD_DIM)), const_spec((1, HEAD_W))]
    grid_spec = pltpu.PrefetchScalarGridSpec(
        num_scalar_prefetch=1, grid=(nb, n_pages // n_chunk), in_specs=in_specs,
        out_specs=pl.BlockSpec((None, n_tok, ATTN_W), lambda b, j, pt: (b, 0, 0)),
        scratch_shapes=[pltpu.VMEM((n_rows, HEAD_W), BF16), pltpu.VMEM((n_rows, 1), F32),
                        pltpu.VMEM((n_rows, 1), F32), pltpu.VMEM((n_rows, HEAD_W), F32),
                        pltpu.VMEM((n_chunk, n_rows, PAGE_SIZE * N_HEADS), F32),
                        pltpu.VMEM((1, n_rows, PAGE_SIZE), F32)],
    )
    body = functools.partial(_decode_attn_body, n_chunk=n_chunk, lam_init=lam_init)
    return pl.pallas_call(
        body, grid_spec=grid_spec, out_shape=jax.ShapeDtypeStruct((nb, n_tok, ATTN_W), F32),
        compiler_params=_params(("arbitrary", "arbitrary")),
    )(page_table, q8, *([cache_k] * n_chunk), *([cache_v] * n_chunk), kn, vn, bias, biasn, lp, sg)


def _pool_mix(u, window_sum, cnt_of, wg_ref, ps_ref):
    outs = []
    for g, w in enumerate(POOL_WINDOWS):
        cols = slice(g * POOL_GROUP, (g + 1) * POOL_GROUP)
        d = window_sum(w, cols) / cnt_of(w) - u[:, cols]
        y = jnp.dot(d.astype(BF16), wg_ref[g], preferred_element_type=F32)
        outs.append(y * ps_ref[:, cols])
    return jnp.concatenate(outs, axis=1)


def _merge_tail(x, oa, za, op, zp, ga, gp, wo_ref):
    ha = (oa * jax.nn.silu(za)).astype(BF16)
    hp = (op * jax.nn.silu(zp)).astype(BF16)
    a = jnp.dot(ha, wo_ref[0:ATTN_W, :], preferred_element_type=F32)
    p = jnp.dot(hp, wo_ref[ATTN_W:ATTN_W + POOL_W, :], preferred_element_type=F32)
    return x + jax.nn.sigmoid(ga) * a + jax.nn.sigmoid(gp) * p


def _merge_prompt_body(x_ref, oa_ref, za_ref, u_ref, halo_ref, zp_ref, ga_ref, gp_ref,
                       wg_ref, ps_ref, wo_ref, y_ref, ext_ref, *, tm):
    i = pl.program_id(1)
    u = u_ref[...]
    ext_ref[0:HALO, :] = jnp.where(i == 0, 0.0, halo_ref[...])
    ext_ref[HALO:HALO + tm, :] = u
    pos = i * tm + lax.broadcasted_iota(jnp.int32, (tm, 1), 0)

    def window_sum(w, cols):
        s = u[:, cols]
        for k in range(1, w):
            s = s + ext_ref[HALO - k:HALO - k + tm, cols]
        return s

    def cnt_of(w):
        return jnp.minimum(w, pos + 1).astype(F32)

    op = _pool_mix(u, window_sum, cnt_of, wg_ref, ps_ref)
    y_ref[...] = _merge_tail(x_ref[...], oa_ref[...], za_ref[...], op, zp_ref[...],
                             ga_ref[...], gp_ref[...], wo_ref)


def _merge_prompt(x, oa, rest, wg, ps, wo, tm):
    b, t, _ = x.shape

    def rest_spec(c):
        return pl.BlockSpec((None, None, tm, D_MODEL), lambda bi, i: (c, bi, i, 0))

    row_spec = pl.BlockSpec((None, tm, D_MODEL), lambda bi, i: (bi, i, 0))
    halo_spec = pl.BlockSpec((None, None, HALO, D_MODEL),
                             lambda bi, i: (1, bi, jnp.maximum(i * (tm // HALO) - 1, 0), 0))
    body = functools.partial(_merge_prompt_body, tm=tm)
    return pl.pallas_call(
        body, grid=(b, t // tm),
        in_specs=[row_spec, row_spec, rest_spec(0), rest_spec(1), halo_spec, rest_spec(2),
                  rest_spec(3), rest_spec(4),
                  pl.BlockSpec((len(POOL_WINDOWS), POOL_GROUP, POOL_GROUP), lambda bi, i: (0, 0, 0)),
                  pl.BlockSpec((1, POOL_W), lambda bi, i: (0, 0)),
                  pl.BlockSpec((ATTN_W + POOL_W, D_MODEL), lambda bi, i: (0, 0))],
        out_specs=row_spec,
        out_shape=jax.ShapeDtypeStruct((b, t, D_MODEL), F32),
        scratch_shapes=[pltpu.VMEM((HALO + tm, D_MODEL), F32)],
        compiler_params=_params(("arbitrary", "arbitrary")),
    )(x, oa, rest, rest, rest, rest, rest, rest, wg, ps, wo)


def _merge_sample_body(x_ref, oa_ref, za_ref, zp_ref, ga_ref, gp_ref, ext_ref,
                       wg_ref, ps_ref, wo_ref, y_ref, *, start_pos):
    t = pl.program_id(0)
    u = ext_ref[POOL_HIST + t]

    def window_sum(w, cols):
        s = u[:, cols]
        for k in range(1, w):
            s = s + ext_ref[POOL_HIST + t - k][:, cols]
        return s

    def cnt_of(w):
        return jnp.minimum(w, start_pos + t + 1).astype(F32)

    op = _pool_mix(u, window_sum, cnt_of, wg_ref, ps_ref)
    y_ref[...] = _merge_tail(x_ref[...], oa_ref[...], za_ref[...], op, zp_ref[...],
                             ga_ref[...], gp_ref[...], wo_ref)


def _merge_sample(x, oa, rest, ext, wg, ps, wo, start_pos):
    t, b, _ = x.shape

    def rest_spec(c):
        return pl.BlockSpec((None, None, b, D_MODEL), lambda ti: (c, ti, 0, 0))

    row_spec = pl.BlockSpec((None, b, D_MODEL), lambda ti: (ti, 0, 0))
    body = functools.partial(_merge_sample_body, start_pos=start_pos)
    return pl.pallas_call(
        body, grid=(t,),
        in_specs=[row_spec, row_spec, rest_spec(0), rest_spec(2), rest_spec(3), rest_spec(4),
                  pl.BlockSpec(ext.shape, lambda ti: (0, 0, 0)),
                  pl.BlockSpec((len(POOL_WINDOWS), POOL_GROUP, POOL_GROUP), lambda ti: (0, 0, 0)),
                  pl.BlockSpec((1, POOL_W), lambda ti: (0, 0)),
                  pl.BlockSpec((ATTN_W + POOL_W, D_MODEL), lambda ti: (0, 0))],
        out_specs=row_spec,
        out_shape=jax.ShapeDtypeStruct((t, b, D_MODEL), F32),
        compiler_params=_params(("arbitrary",)),
    )(x, oa, rest, rest, rest, rest, ext, wg, ps, wo)


Q_SCALE = ATTN_SCALE * math.log2(math.e)
PROJ_ROWS = 256
PROMPT_ATTN_TILE = 512
PROMPT_MERGE_ROWS = 512
DECODE_PAGES_PER_STEP = 16


def kernel(x_prompt, x_sample, cache_k, cache_v, state_pool, page_table, g_norm, w_in, q_norm_g,
           k_norm_g, lambda_params, subln_g, w_grp, pool_scale, w_out):
    depth = g_norm.shape[0]
    assert depth == 1
    l = 0
    lam_init = 0.8 - 0.6 * math.exp(-0.3 * l)
    bp, tp, _ = x_prompt.shape
    bs, ts, _ = x_sample.shape
    assert ts * N_MAPS == ROWS_PER_HEAD
    past_len = page_table.shape[1] * PAGE_SIZE

    w_bf16 = w_in[l].astype(BF16)
    wo = w_out[l].astype(BF16)
    wg = w_grp[l].astype(BF16)
    gn = g_norm[l].reshape(1, D_MODEL)
    gq = jnp.tile(q_norm_g[l], N_HEADS * N_MAPS).reshape(1, ATTN_W)
    gk = jnp.tile(k_norm_g[l], N_HEADS * N_MAPS).reshape(1, ATTN_W)
    grp = jnp.arange(ATTN_W) // HEAD_DIM
    gmat = jnp.where(grp[:, None] == grp[None, :], 1.0 / HEAD_DIM, 0.0).astype(BF16)
    lp = lambda_params[l]
    sg = subln_g[l].reshape(1, HEAD_W)
    ps = pool_scale[l].reshape(1, POOL_W)

    xp2d = x_prompt.reshape(bp * tp, D_MODEL)
    q, k32, k16, v32, v16, rest = _in_proj(xp2d, gn, w_bf16, gq, gk, gmat, PROJ_ROWS, Q_SCALE, BF16)
    tq = PROMPT_ATTN_TILE
    qt = jnp.swapaxes(q.reshape(bp, tp, ATTN_W), 1, 2)
    vt = v16.reshape(bp, tp // tq, tq, N_HEADS, HEAD_W).transpose(0, 3, 1, 4, 2)
    oa = _prompt_attn(qt, k16.reshape(bp, tp, ATTN_W), vt, lp, sg.reshape(HEAD_W, 1), lam_init, tq)
    rest = rest.reshape(5, bp, tp, D_MODEL)
    y_prompt = _merge_prompt(x_prompt, oa, rest, wg, ps, wo, PROMPT_MERGE_ROWS)
    k_prompt = k32.reshape(1, bp, tp, N_HEADS, HEAD_W)
    v_prompt = v32.reshape(1, bp, tp, N_HEADS, HEAD_W)
    pool_prompt = rest[1][:, tp - POOL_HIST:][None]

    xs_tm = jnp.swapaxes(x_sample, 0, 1)
    qs, ks32, _, vs32, _, rest_s = _in_proj(xs_tm.reshape(ts * bs, D_MODEL), gn, w_bf16, gq, gk, gmat,
                                            PROJ_ROWS, Q_SCALE, F32)

    def seq_major(a):
        return jnp.swapaxes(a.reshape(ts, bs, ATTN_W), 0, 1)

    qs, ks, vs = seq_major(qs), seq_major(ks32), seq_major(vs32)
    q8 = jnp.concatenate([qs, qs], axis=1)
    kn = ks.reshape(bs, ts * N_HEADS, HEAD_W)
    vn = vs.reshape(bs, ts * N_HEADS, HEAD_W)
    oa_s = _decode_attn(page_table, q8, cache_k, cache_v, kn, vn, lp, sg, lam_init, DECODE_PAGES_PER_STEP)
    rest_s = rest_s.reshape(5, ts, bs, D_MODEL)
    ext = jnp.concatenate([jnp.swapaxes(state_pool[l], 0, 1), rest_s[1]], axis=0)
    y_s = _merge_sample(xs_tm, jnp.swapaxes(oa_s, 0, 1), rest_s, ext, wg, ps, wo, past_len)
    y_sample = jnp.swapaxes(y_s, 0, 1)
    k_sample = ks.reshape(1, bs, ts, N_HEADS, HEAD_W)
    v_sample = vs.reshape(1, bs, ts, N_HEADS, HEAD_W)
    u_s = jnp.swapaxes(rest_s[1], 0, 1)
    pool_sample = jnp.concatenate([state_pool[l][:, ts:], u_s], axis=1)[None]

    return (y_prompt, y_sample, k_prompt, v_prompt, pool_prompt, k_sample, v_sample, pool_sample)
```

```python
import functools
import math

import jax
import jax.numpy as jnp
from jax import lax
from jax.experimental import pallas as pl
from jax.experimental.pallas import tpu as pltpu

D_MODEL = 1024
N_HEADS = 8
HEAD_DIM = 64
HEAD_W = 2 * HEAD_DIM
ATTN_W = N_HEADS * HEAD_W
POOL_WINDOWS = (2, 4, 8, 16)
POOL_W = D_MODEL
POOL_GROUP = POOL_W // len(POOL_WINDOWS)
POOL_HIST = max(POOL_WINDOWS) - 1
SUBLANES = 8
HALO = SUBLANES * len(POOL_WINDOWS)
assert HALO >= POOL_HIST + 1
PAGE_SIZE = 128
ATTN_SCALE = HEAD_DIM ** -0.5
NEG_INF = -1e30
NORM_EPS = 1e-6
N_MAPS = 2
ROWS_PER_HEAD = 8

F32 = jnp.float32
BF16 = jnp.bfloat16
VMEM_LIMIT = 56 * 1024 * 1024


def _params(sem):
    return pltpu.CompilerParams(dimension_semantics=sem, vmem_limit_bytes=VMEM_LIMIT)


N_REST = 5


def _in_proj_body(x_ref, g_ref, w_ref, gq_ref, gk_ref, gmat_ref,
                  q_ref, k32_ref, k16_ref, v32_ref, v16_ref, rest_ref, *, q_scale):
    x = x_ref[...]
    ms = jnp.mean(x * x, axis=-1, keepdims=True)
    xn = (x * lax.rsqrt(ms + NORM_EPS) * g_ref[...]).astype(BF16)

    def group(g):
        return jnp.dot(xn, w_ref[:, g * D_MODEL:(g + 1) * D_MODEL], preferred_element_type=F32)

    def head_norm(p, gain_ref):
        gm = jnp.dot((p * p).astype(BF16), gmat_ref[...], preferred_element_type=F32)
        return p * lax.rsqrt(gm + NORM_EPS) * gain_ref[...]

    q_ref[...] = (head_norm(group(0), gq_ref) * q_scale).astype(q_ref.dtype)
    k = head_norm(group(1), gk_ref)
    k32_ref[...] = k
    k16_ref[...] = k.astype(BF16)
    v = group(2)
    v32_ref[...] = v
    v16_ref[...] = v.astype(BF16)
    for r in range(N_REST):
        rest_ref[r] = group(3 + r)


def _in_proj(x2d, g_norm, w_bf16, gq, gk, gmat, tm, q_scale, q_dtype):
    m = x2d.shape[0]
    in_w = w_bf16.shape[1]

    def resident(shape):
        return pl.BlockSpec(shape, lambda i: (0,) * len(shape), pipeline_mode=pl.Buffered(1))

    row = pl.BlockSpec((tm, D_MODEL), lambda i: (i, 0))
    outs = [(q_dtype, row), (F32, row), (BF16, row), (F32, row), (BF16, row)]
    out_shape = [jax.ShapeDtypeStruct((m, D_MODEL), dt) for dt, _ in outs]
    out_shape.append(jax.ShapeDtypeStruct((N_REST, m, D_MODEL), F32))
    out_specs = [spec for _, spec in outs] + [pl.BlockSpec((N_REST, tm, D_MODEL), lambda i: (0, i, 0))]
    return pl.pallas_call(
        functools.partial(_in_proj_body, q_scale=q_scale), grid=(m // tm,),
        in_specs=[row, resident((1, D_MODEL)), resident((D_MODEL, in_w)), resident((1, ATTN_W)),
                  resident((1, ATTN_W)), resident((ATTN_W, ATTN_W))],
        out_specs=out_specs, out_shape=out_shape,
        compiler_params=_params(("arbitrary",)),
    )(x2d, g_norm, w_bf16, gq, gk, gmat)


def _lam(lp_ref, lam_init):
    lp = lp_ref[...]
    a = jnp.sum(lp[0:1] * lp[1:2], axis=-1, keepdims=True)
    b = jnp.sum(lp[2:3] * lp[3:4], axis=-1, keepdims=True)
    return jnp.exp(a) - jnp.exp(b) + lam_init


def _prompt_attn_body(qt_ref, k_ref, vt_ref, lp_ref, sg_ref, o_ref, acc_ref, s_ref, p_ref, *, tq, lam_init):
    i = pl.program_id(2)
    qt = qt_ref[...].astype(F32)
    row = lax.broadcasted_iota(jnp.int32, qt.shape, 0)
    wq = [jnp.where(row < HEAD_DIM, qt, 0.0).astype(BF16), jnp.where(row >= HEAD_DIM, qt, 0.0).astype(BF16)]
    acc_ref[...] = jnp.zeros(acc_ref.shape, F32)

    def scores(j):
        start = pl.multiple_of(j * tq, tq)
        kj = k_ref[pl.ds(start, tq), :]
        for c in range(N_MAPS):
            s_ref[c] = jnp.dot(kj, wq[c], preferred_element_type=F32)

    def softmax(m, l, masked):
        out = []
        for c in range(N_MAPS):
            s = s_ref[c]
            if masked:
                key = lax.broadcasted_iota(jnp.int32, s.shape, 0)
                qry = lax.broadcasted_iota(jnp.int32, s.shape, 1)
                s = jnp.where(key <= qry, s, NEG_INF)
            m_new = jnp.maximum(m[c], jnp.max(s, axis=0, keepdims=True))
            alpha = jnp.exp2(m[c] - m_new)
            p = jnp.exp2(s - m_new)
            p_ref[c] = p.astype(BF16)
            out.append((m_new, alpha * l[c] + jnp.sum(p, axis=0, keepdims=True), alpha))
        return [o[0] for o in out], [o[1] for o in out], [o[2] for o in out]

    def pv_update(j, alpha):
        vtj = vt_ref[j]
        for c in range(N_MAPS):
            acc_ref[c] = alpha[c] * acc_ref[c] + jnp.dot(vtj, p_ref[c], preferred_element_type=F32)

    def body(j, carry):
        m, l, alpha_prev = carry
        pv_update(jnp.maximum(j - 1, 0), alpha_prev)
        m, l, alpha = softmax(m, l, False)
        scores(j + 1)
        return m, l, alpha

    scores(0)
    p_ref[...] = jnp.zeros(p_ref.shape, BF16)
    init = ([jnp.full((1, tq), NEG_INF, F32)] * N_MAPS, [jnp.zeros((1, tq), F32)] * N_MAPS,
            [jnp.ones((1, tq), F32)] * N_MAPS)
    m, l, alpha_prev = lax.fori_loop(0, i, body, init)
    pv_update(jnp.maximum(i - 1, 0), alpha_prev)
    m, (l0, l1), alpha = softmax(m, l, True)
    pv_update(i, alpha)
    o = acc_ref[0] / l0 - _lam(lp_ref, lam_init) * (acc_ref[1] / l1)
    ms = jnp.mean(o * o, axis=0, keepdims=True)
    o = o * lax.rsqrt(ms + NORM_EPS) * sg_ref[...] * (1.0 - lam_init)
    o_ref[...] = o.T


def _prompt_attn(qt, k, vt, lp, sg_col, lam_init, tq):
    b, t, _ = k.shape
    body = functools.partial(_prompt_attn_body, tq=tq, lam_init=lam_init)
    return pl.pallas_call(
        body, grid=(b, N_HEADS, t // tq),
        in_specs=[
            pl.BlockSpec((None, HEAD_W, tq), lambda bi, h, i: (bi, h, i)),
            pl.BlockSpec((None, t, HEAD_W), lambda bi, h, i: (bi, 0, h)),
            pl.BlockSpec((None, None, t // tq, HEAD_W, tq), lambda bi, h, i: (bi, h, 0, 0, 0)),
            pl.BlockSpec((4, HEAD_DIM), lambda bi, h, i: (0, 0)),
            pl.BlockSpec((HEAD_W, 1), lambda bi, h, i: (0, 0)),
        ],
        out_specs=pl.BlockSpec((None, tq, HEAD_W), lambda bi, h, i: (bi, i, h)),
        out_shape=jax.ShapeDtypeStruct((b, t, ATTN_W), F32),
        scratch_shapes=[pltpu.VMEM((N_MAPS, HEAD_W, tq), F32), pltpu.VMEM((N_MAPS, tq, tq), F32),
                        pltpu.VMEM((N_MAPS, tq, tq), BF16)],
        compiler_params=_params(("arbitrary", "arbitrary", "arbitrary")),
    )(qt, k, vt, lp, sg_col)


def _decode_attn_body(pt_ref, q_ref, *refs, n_chunk, lam_init):
    del pt_ref
    k_refs = refs[:n_chunk]
    v_refs = refs[n_chunk:2 * n_chunk]
    (kn_ref, vn_ref, bias_ref, biasn_ref, lp_ref, sg_ref, o_ref,
     qm_ref, m_ref, l_ref, acc_ref, s_ref, sn_ref) = refs[2 * n_chunk:]
    j = pl.program_id(1)
    n_tok = ROWS_PER_HEAD // N_MAPS
    page_rows = PAGE_SIZE * N_HEADS

    @pl.when(j == 0)
    def _():
        q8 = q_ref[...]
        row = lax.broadcasted_iota(jnp.int32, (ROWS_PER_HEAD, HEAD_W), 0)
        lane = lax.broadcasted_iota(jnp.int32, (ROWS_PER_HEAD, HEAD_W), 1)
        keep = (lane // HEAD_DIM) == (row // n_tok)
        blocks = [jnp.where(keep, q8[:, h * HEAD_W:(h + 1) * HEAD_W], 0.0) for h in range(N_HEADS)]
        qm_ref[...] = jnp.concatenate(blocks, axis=0).astype(BF16)
        m_ref[...] = jnp.full(m_ref.shape, NEG_INF, F32)
        l_ref[...] = jnp.zeros(l_ref.shape, F32)
        acc_ref[...] = jnp.zeros(acc_ref.shape, F32)

    qm = qm_ref[...]

    def update(state, kmats, vmats, bias, stage_ref):
        m, l, acc = state
        m_new = m
        for c, kmat in enumerate(kmats):
            s = lax.dot_general(qm, kmat.astype(BF16), (((1,), (1,)), ((), ())),
                                preferred_element_type=F32) + bias
            stage_ref[c] = s
            m_new = jnp.maximum(m_new, jnp.max(s, axis=-1, keepdims=True))
        alpha = jnp.exp2(m - m_new)
        l = alpha * l
        acc = alpha * acc
        for c, vmat in enumerate(vmats):
            p = jnp.exp2(stage_ref[c] - m_new)
            l = l + jnp.sum(p, axis=-1, keepdims=True)
            acc = acc + jnp.dot(p.astype(BF16), vmat.astype(BF16), preferred_element_type=F32)
        return m_new, l, acc

    state = update((m_ref[...], l_ref[...], acc_ref[...]),
                   [k_refs[c][...].reshape(page_rows, HEAD_W) for c in range(n_chunk)],
                   [v_refs[c][...].reshape(page_rows, HEAD_W) for c in range(n_chunk)],
                   bias_ref[...], s_ref)
    m_ref[...], l_ref[...], acc_ref[...] = state

    @pl.when(j == pl.num_programs(1) - 1)
    def _():
        pad = jnp.zeros((PAGE_SIZE - kn_ref.shape[0], HEAD_W), F32)
        kn = jnp.concatenate([kn_ref[...], pad], axis=0)
        vn = jnp.concatenate([vn_ref[...], pad], axis=0)
        _, l, acc = update(state, [kn], [vn], biasn_ref[...], sn_ref)
        o = acc / l
        lam = _lam(lp_ref, lam_init)
        for h in range(N_HEADS):
            blk = o[h * ROWS_PER_HEAD:(h + 1) * ROWS_PER_HEAD]
            r = blk[:n_tok] - lam * blk[n_tok:]
            ms = jnp.mean(r * r, axis=-1, keepdims=True)
            o_ref[:, h * HEAD_W:(h + 1) * HEAD_W] = (
                r * lax.rsqrt(ms + NORM_EPS) * sg_ref[...] * (1.0 - lam_init))


def _decode_bias(n_tok):
    n = jnp.arange(N_HEADS * ROWS_PER_HEAD)[:, None]
    col = jnp.arange(PAGE_SIZE * N_HEADS)[None, :]
    same_head = (col % N_HEADS) == (n // ROWS_PER_HEAD)
    bias = jnp.where(same_head, 0.0, NEG_INF).astype(F32)
    coln = jnp.arange(PAGE_SIZE)[None, :]
    ok = (coln < n_tok * N_HEADS) & ((coln % N_HEADS) == (n // ROWS_PER_HEAD)) & \
         ((coln // N_HEADS) <= (n % n_tok))
    return bias, jnp.where(ok, 0.0, NEG_INF).astype(F32)


def _decode_attn(page_table, q8, cache_k, cache_v, kn, vn, lp, sg, lam_init, n_chunk):
    nb, n_pages = page_table.shape
    n_tok = ROWS_PER_HEAD // N_MAPS
    n_rows = N_HEADS * ROWS_PER_HEAD
    bias, biasn = _decode_bias(n_tok)

    def page_spec(c):
        return pl.BlockSpec((None, None, PAGE_SIZE, N_HEADS, HEAD_W),
                            lambda b, j, pt: (0, pt[b, j * n_chunk + c], 0, 0, 0))

    def const_spec(shape):
        return pl.BlockSpec(shape, lambda b, j, pt: (0,) * len(shape))

    new_spec = pl.BlockSpec((None, n_tok * N_HEADS, HEAD_W), lambda b, j, pt: (b, 0, 0))
    in_specs = [pl.BlockSpec((None, ROWS_PER_HEAD, ATTN_W), lambda b, j, pt: (b, 0, 0))]
    in_specs += [page_spec(c) for c in range(n_chunk)]
    in_specs += [page_spec(c) for c in range(n_chunk)]
    in_specs += [new_spec, new_spec, const_spec(bias.shape), const_spec(biasn.shape),
                 const_spec((4, HEAD_DIM)), const_spec((1, HEAD_W))]
    grid_spec = pltpu.PrefetchScalarGridSpec(
        num_scalar_prefetch=1, grid=(nb, n_pages // n_chunk), in_specs=in_specs,
        out_specs=pl.BlockSpec((None, n_tok, ATTN_W), lambda b, j, pt: (b, 0, 0)),
        scratch_shapes=[pltpu.VMEM((n_rows, HEAD_W), BF16), pltpu.VMEM((n_rows, 1), F32),
                        pltpu.VMEM((n_rows, 1), F32), pltpu.VMEM((n_rows, HEAD_W), F32),
                        pltpu.VMEM((n_chunk, n_rows, PAGE_SIZE * N_HEADS), F32),
                        pltpu.VMEM((1, n_rows, PAGE_SIZE), F32)],
    )
    body = functools.partial(_decode_attn_body, n_chunk=n_chunk, lam_init=lam_init)
    return pl.pallas_call(
        body, grid_spec=grid_spec, out_shape=jax.ShapeDtypeStruct((nb, n_tok, ATTN_W), F32),
        compiler_params=_params(("arbitrary", "arbitrary")),
    )(page_table, q8, *([cache_k] * n_chunk), *([cache_v] * n_chunk), kn, vn, bias, biasn, lp, sg)


def _pool_mix(u, window_sum, cnt_of, wg_ref, ps_ref):
    outs = []
    for g, w in enumerate(POOL_WINDOWS):
        cols = slice(g * POOL_GROUP, (g + 1) * POOL_GROUP)
        d = window_sum(g, w, cols) * (1.0 / cnt_of(w)) - u[:, cols]
        y = jnp.dot(d.astype(BF16), wg_ref[g], preferred_element_type=F32)
        outs.append(y * ps_ref[:, cols])
    return jnp.concatenate(outs, axis=1)


def _sigmoid(x):
    return 0.5 * jnp.tanh(0.5 * x) + 0.5


def _merge_tail(x, oa, za, op, zp, ga, gp, wo_ref):
    ha = (oa * (za * _sigmoid(za))).astype(BF16)
    hp = (op * (zp * _sigmoid(zp))).astype(BF16)
    a = jnp.dot(ha, wo_ref[0:ATTN_W, :], preferred_element_type=F32)
    p = jnp.dot(hp, wo_ref[ATTN_W:ATTN_W + POOL_W, :], preferred_element_type=F32)
    return x + _sigmoid(ga) * a + _sigmoid(gp) * p


def _merge_prompt_body(x_ref, oa_ref, za_ref, u_ref, halo_ref, zp_ref, ga_ref, gp_ref,
                       wg_ref, ps_ref, wo_ref, y_ref, ext_ref, lvl_ref, *, tm):
    i = pl.program_id(1)
    u = u_ref[...]
    ext_ref[0:HALO, :] = jnp.where(i == 0, 0.0, halo_ref[...])
    ext_ref[HALO:HALO + tm, :] = u
    pos = i * tm + lax.broadcasted_iota(jnp.int32, (tm, 1), 0)

    def window_sum(g, w, cols):
        src, src_cols = ext_ref, cols
        for lvl in range(g + 1):
            shift = 1 << lvl
            start = HALO if lvl == g else SUBLANES * (lvl + 1)
            n = HALO + tm - start
            val = src[start:start + n, src_cols] + src[start - shift:start - shift + n, src_cols]
            if lvl == g:
                return val
            lvl_ref[lvl, start:start + n, :] = val
            src, src_cols = lvl_ref.at[lvl], slice(None)

    def cnt_of(w):
        return jnp.minimum(w, pos + 1).astype(F32)

    op = _pool_mix(u, window_sum, cnt_of, wg_ref, ps_ref)
    y_ref[...] = _merge_tail(x_ref[...], oa_ref[...], za_ref[...], op, zp_ref[...],
                             ga_ref[...], gp_ref[...], wo_ref)


def _merge_prompt(x, oa, rest, wg, ps, wo, tm):
    b, t, _ = x.shape

    def rest_spec(c):
        return pl.BlockSpec((None, None, tm, D_MODEL), lambda bi, i: (c, bi, i, 0))

    row_spec = pl.BlockSpec((None, tm, D_MODEL), lambda bi, i: (bi, i, 0))
    halo_spec = pl.BlockSpec((None, None, HALO, D_MODEL),
                             lambda bi, i: (1, bi, jnp.maximum(i * (tm // HALO) - 1, 0), 0))
    body = functools.partial(_merge_prompt_body, tm=tm)
    return pl.pallas_call(
        body, grid=(b, t // tm),
        in_specs=[row_spec, row_spec, rest_spec(0), rest_spec(1), halo_spec, rest_spec(2),
                  rest_spec(3), rest_spec(4),
                  pl.BlockSpec((len(POOL_WINDOWS), POOL_GROUP, POOL_GROUP), lambda bi, i: (0, 0, 0)),
                  pl.BlockSpec((1, POOL_W), lambda bi, i: (0, 0)),
                  pl.BlockSpec((ATTN_W + POOL_W, D_MODEL), lambda bi, i: (0, 0))],
        out_specs=row_spec,
        out_shape=jax.ShapeDtypeStruct((b, t, D_MODEL), F32),
        scratch_shapes=[pltpu.VMEM((HALO + tm, D_MODEL), F32),
                        pltpu.VMEM((len(POOL_WINDOWS) - 1, HALO + tm, POOL_GROUP), F32)],
        compiler_params=_params(("arbitrary", "arbitrary")),
    )(x, oa, rest, rest, rest, rest, rest, rest, wg, ps, wo)


def _merge_sample_body(x_ref, oa_ref, za_ref, zp_ref, ga_ref, gp_ref, ext_ref,
                       wg_ref, ps_ref, wo_ref, y_ref, *, start_pos):
    t = pl.program_id(0)
    u = ext_ref[POOL_HIST + t]

    def window_sum(g, w, cols):
        del g
        s = u[:, cols]
        for k in range(1, w):
            s = s + ext_ref[POOL_HIST + t - k][:, cols]
        return s

    def cnt_of(w):
        return jnp.minimum(w, start_pos + t + 1).astype(F32)

    op = _pool_mix(u, window_sum, cnt_of, wg_ref, ps_ref)
    y_ref[...] = _merge_tail(x_ref[...], oa_ref[...], za_ref[...], op, zp_ref[...],
                             ga_ref[...], gp_ref[...], wo_ref)


def _merge_sample(x, oa, rest, ext, wg, ps, wo, start_pos):
    t, b, _ = x.shape

    def rest_spec(c):
        return pl.BlockSpec((None, None, b, D_MODEL), lambda ti: (c, ti, 0, 0))

    row_spec = pl.BlockSpec((None, b, D_MODEL), lambda ti: (ti, 0, 0))
    body = functools.partial(_merge_sample_body, start_pos=start_pos)
    return pl.pallas_call(
        body, grid=(t,),
        in_specs=[row_spec, row_spec, rest_spec(0), rest_spec(2), rest_spec(3), rest_spec(4),
                  pl.BlockSpec(ext.shape, lambda ti: (0, 0, 0)),
                  pl.BlockSpec((len(POOL_WINDOWS), POOL_GROUP, POOL_GROUP), lambda ti: (0, 0, 0)),
                  pl.BlockSpec((1, POOL_W), lambda ti: (0, 0)),
                  pl.BlockSpec((ATTN_W + POOL_W, D_MODEL), lambda ti: (0, 0))],
        out_specs=row_spec,
        out_shape=jax.ShapeDtypeStruct((t, b, D_MODEL), F32),
        compiler_params=_params(("arbitrary",)),
    )(x, oa, rest, rest, rest, rest, ext, wg, ps, wo)


Q_SCALE = ATTN_SCALE * math.log2(math.e)
PROJ_ROWS = 256
PROMPT_ATTN_TILE = 512
PROMPT_MERGE_ROWS = 512
DECODE_PAGES_PER_STEP = 16


def kernel(x_prompt, x_sample, cache_k, cache_v, state_pool, page_table, g_norm, w_in, q_norm_g,
           k_norm_g, lambda_params, subln_g, w_grp, pool_scale, w_out):
    depth = g_norm.shape[0]
    assert depth == 1
    l = 0
    lam_init = 0.8 - 0.6 * math.exp(-0.3 * l)
    bp, tp, _ = x_prompt.shape
    bs, ts, _ = x_sample.shape
    assert ts * N_MAPS == ROWS_PER_HEAD
    past_len = page_table.shape[1] * PAGE_SIZE

    w_bf16 = w_in[l].astype(BF16)
    wo = w_out[l].astype(BF16)
    wg = w_grp[l].astype(BF16)
    gn = g_norm[l].reshape(1, D_MODEL)
    gq = jnp.tile(q_norm_g[l], N_HEADS * N_MAPS).reshape(1, ATTN_W)
    gk = jnp.tile(k_norm_g[l], N_HEADS * N_MAPS).reshape(1, ATTN_W)
    grp = jnp.arange(ATTN_W) // HEAD_DIM
    gmat = jnp.where(grp[:, None] == grp[None, :], 1.0 / HEAD_DIM, 0.0).astype(BF16)
    lp = lambda_params[l]
    sg = subln_g[l].reshape(1, HEAD_W)
    ps = pool_scale[l].reshape(1, POOL_W)

    xp2d = x_prompt.reshape(bp * tp, D_MODEL)
    q, k32, k16, v32, v16, rest = _in_proj(xp2d, gn, w_bf16, gq, gk, gmat, PROJ_ROWS, Q_SCALE, BF16)
    tq = PROMPT_ATTN_TILE
    qt = jnp.swapaxes(q.reshape(bp, tp, ATTN_W), 1, 2)
    vt = v16.reshape(bp, tp // tq, tq, N_HEADS, HEAD_W).transpose(0, 3, 1, 4, 2)
    oa = _prompt_attn(qt, k16.reshape(bp, tp, ATTN_W), vt, lp, sg.reshape(HEAD_W, 1), lam_init, tq)
    rest = rest.reshape(5, bp, tp, D_MODEL)
    y_prompt = _merge_prompt(x_prompt, oa, rest, wg, ps, wo, PROMPT_MERGE_ROWS)
    k_prompt = k32.reshape(1, bp, tp, N_HEADS, HEAD_W)
    v_prompt = v32.reshape(1, bp, tp, N_HEADS, HEAD_W)
    pool_prompt = lax.slice(rest, (1, 0, tp - POOL_HIST, 0), (2, bp, tp, D_MODEL))

    xs_tm = jnp.swapaxes(x_sample, 0, 1)
    qs, ks32, _, vs32, _, rest_s = _in_proj(xs_tm.reshape(ts * bs, D_MODEL), gn, w_bf16, gq, gk, gmat,
                                            PROJ_ROWS, Q_SCALE, F32)

    def seq_major(a):
        return jnp.swapaxes(a.reshape(ts, bs, ATTN_W), 0, 1)

    qs, ks, vs = seq_major(qs), seq_major(ks32), seq_major(vs32)
    q8 = jnp.concatenate([qs, qs], axis=1)
    kn = ks.reshape(bs, ts * N_HEADS, HEAD_W)
    vn = vs.reshape(bs, ts * N_HEADS, HEAD_W)
    oa_s = _decode_attn(page_table, q8, cache_k, cache_v, kn, vn, lp, sg, lam_init, DECODE_PAGES_PER_STEP)
    rest_s = rest_s.reshape(5, ts, bs, D_MODEL)
    ext = jnp.concatenate([jnp.swapaxes(state_pool[l], 0, 1), rest_s[1]], axis=0)
    y_s = _merge_sample(xs_tm, jnp.swapaxes(oa_s, 0, 1), rest_s, ext, wg, ps, wo, past_len)
    y_sample = jnp.swapaxes(y_s, 0, 1)
    k_sample = ks.reshape(1, bs, ts, N_HEADS, HEAD_W)
    v_sample = vs.reshape(1, bs, ts, N_HEADS, HEAD_W)
    u_s = jnp.swapaxes(rest_s[1], 0, 1)
    pool_sample = jnp.concatenate([state_pool[l][:, ts:], u_s], axis=1)[None]

    return (y_prompt, y_sample, k_prompt, v_prompt, pool_prompt, k_sample, v_sample, pool_sample)
```

```python
import functools
import math

import jax
import jax.numpy as jnp
from jax import lax
from jax.experimental import pallas as pl
from jax.experimental.pallas import tpu as pltpu

D_MODEL = 1024
N_HEADS = 8
HEAD_DIM = 64
HEAD_W = 2 * HEAD_DIM
ATTN_W = N_HEADS * HEAD_W
POOL_WINDOWS = (2, 4, 8, 16)
POOL_W = D_MODEL
POOL_GROUP = POOL_W // len(POOL_WINDOWS)
POOL_HIST = max(POOL_WINDOWS) - 1
SUBLANES = 8
HALO = SUBLANES * len(POOL_WINDOWS)
assert HALO >= POOL_HIST + 1
PAGE_SIZE = 128
ATTN_SCALE = HEAD_DIM ** -0.5
NEG_INF = -1e30
NORM_EPS = 1e-6
N_MAPS = 2
ROWS_PER_HEAD = 8

F32 = jnp.float32
BF16 = jnp.bfloat16
VMEM_LIMIT = 56 * 1024 * 1024


def _params(sem):
    return pltpu.CompilerParams(dimension_semantics=sem, vmem_limit_bytes=VMEM_LIMIT)


U_GROUP = 4
GATE_GROUPS = (3, 5, 6, 7)


def _in_proj_body(x_ref, g_ref, w_ref, gq_ref, gk_ref, gmat_ref,
                  q_ref, k32_ref, k16_ref, v32_ref, v16_ref, u_ref, gate_ref, *, q_scale):
    x = x_ref[...]
    ms = jnp.mean(x * x, axis=-1, keepdims=True)
    xn = (x * lax.rsqrt(ms + NORM_EPS) * g_ref[...]).astype(BF16)

    def group(g):
        return jnp.dot(xn, w_ref[:, g * D_MODEL:(g + 1) * D_MODEL], preferred_element_type=F32)

    def head_norm(p, gain_ref):
        gm = jnp.dot((p * p).astype(BF16), gmat_ref[...], preferred_element_type=F32)
        return p * lax.rsqrt(gm + NORM_EPS) * gain_ref[...]

    q_ref[...] = (head_norm(group(0), gq_ref) * q_scale).astype(q_ref.dtype)
    k = head_norm(group(1), gk_ref)
    k32_ref[...] = k
    k16_ref[...] = k.astype(BF16)
    v = group(2)
    v32_ref[...] = v
    v16_ref[...] = v.astype(BF16)
    u_ref[...] = group(U_GROUP)
    for r, g in enumerate(GATE_GROUPS):
        gate_ref[r] = group(g).astype(BF16)


def _in_proj(x2d, g_norm, w_bf16, gq, gk, gmat, tm, q_scale, q_dtype):
    m = x2d.shape[0]
    in_w = w_bf16.shape[1]

    def resident(shape):
        return pl.BlockSpec(shape, lambda i: (0,) * len(shape), pipeline_mode=pl.Buffered(1))

    row = pl.BlockSpec((tm, D_MODEL), lambda i: (i, 0))
    outs = [(q_dtype, row), (F32, row), (BF16, row), (F32, row), (BF16, row), (F32, row)]
    out_shape = [jax.ShapeDtypeStruct((m, D_MODEL), dt) for dt, _ in outs]
    out_shape.append(jax.ShapeDtypeStruct((len(GATE_GROUPS), m, D_MODEL), BF16))
    out_specs = [spec for _, spec in outs] + [pl.BlockSpec((len(GATE_GROUPS), tm, D_MODEL), lambda i: (0, i, 0))]
    return pl.pallas_call(
        functools.partial(_in_proj_body, q_scale=q_scale), grid=(m // tm,),
        in_specs=[row, resident((1, D_MODEL)), resident((D_MODEL, in_w)), resident((1, ATTN_W)),
                  resident((1, ATTN_W)), resident((ATTN_W, ATTN_W))],
        out_specs=out_specs, out_shape=out_shape,
        compiler_params=_params(("arbitrary",)),
    )(x2d, g_norm, w_bf16, gq, gk, gmat)


def _lam(lp_ref, lam_init):
    lp = lp_ref[...]
    a = jnp.sum(lp[0:1] * lp[1:2], axis=-1, keepdims=True)
    b = jnp.sum(lp[2:3] * lp[3:4], axis=-1, keepdims=True)
    return jnp.exp(a) - jnp.exp(b) + lam_init


def _prompt_attn_body(qt_ref, k_ref, vt_ref, lp_ref, sg_ref, o_ref, acc_ref, s_ref, p_ref, *, tq, lam_init):
    i = pl.program_id(2)
    qt = qt_ref[...].astype(F32)
    row = lax.broadcasted_iota(jnp.int32, qt.shape, 0)
    wq = [jnp.where(row < HEAD_DIM, qt, 0.0).astype(BF16), jnp.where(row >= HEAD_DIM, qt, 0.0).astype(BF16)]
    acc_ref[...] = jnp.zeros(acc_ref.shape, F32)

    def scores(j):
        start = pl.multiple_of(j * tq, tq)
        kj = k_ref[pl.ds(start, tq), :]
        for c in range(N_MAPS):
            s_ref[c] = jnp.dot(kj, wq[c], preferred_element_type=F32)

    def softmax(m, l, masked):
        out = []
        for c in range(N_MAPS):
            s = s_ref[c]
            if masked:
                key = lax.broadcasted_iota(jnp.int32, s.shape, 0)
                qry = lax.broadcasted_iota(jnp.int32, s.shape, 1)
                s = jnp.where(key <= qry, s, NEG_INF)
            m_new = jnp.maximum(m[c], jnp.max(s, axis=0, keepdims=True))
            alpha = jnp.exp2(m[c] - m_new)
            p = jnp.exp2(s - m_new)
            p_ref[c] = p.astype(BF16)
            out.append((m_new, alpha * l[c] + jnp.sum(p, axis=0, keepdims=True), alpha))
        return [o[0] for o in out], [o[1] for o in out], [o[2] for o in out]

    def pv_update(j, alpha):
        vtj = vt_ref[j]
        for c in range(N_MAPS):
            acc_ref[c] = alpha[c] * acc_ref[c] + jnp.dot(vtj, p_ref[c], preferred_element_type=F32)

    def body(j, carry):
        m, l, alpha_prev = carry
        pv_update(jnp.maximum(j - 1, 0), alpha_prev)
        m, l, alpha = softmax(m, l, False)
        scores(j + 1)
        return m, l, alpha

    scores(0)
    p_ref[...] = jnp.zeros(p_ref.shape, BF16)
    init = ([jnp.full((1, tq), NEG_INF, F32)] * N_MAPS, [jnp.zeros((1, tq), F32)] * N_MAPS,
            [jnp.ones((1, tq), F32)] * N_MAPS)
    m, l, alpha_prev = lax.fori_loop(0, i, body, init)
    pv_update(jnp.maximum(i - 1, 0), alpha_prev)
    m, (l0, l1), alpha = softmax(m, l, True)
    pv_update(i, alpha)
    o = acc_ref[0] / l0 - _lam(lp_ref, lam_init) * (acc_ref[1] / l1)
    ms = jnp.mean(o * o, axis=0, keepdims=True)
    o = o * lax.rsqrt(ms + NORM_EPS) * sg_ref[...] * (1.0 - lam_init)
    o_ref[...] = o.T.astype(o_ref.dtype)


def _prompt_attn(qt, k, vt, lp, sg_col, lam_init, tq):
    b, t, _ = k.shape
    body = functools.partial(_prompt_attn_body, tq=tq, lam_init=lam_init)
    return pl.pallas_call(
        body, grid=(b, N_HEADS, t // tq),
        in_specs=[
            pl.BlockSpec((None, HEAD_W, tq), lambda bi, h, i: (bi, h, i)),
            pl.BlockSpec((None, t, HEAD_W), lambda bi, h, i: (bi, 0, h)),
            pl.BlockSpec((None, None, t // tq, HEAD_W, tq), lambda bi, h, i: (bi, h, 0, 0, 0)),
            pl.BlockSpec((4, HEAD_DIM), lambda bi, h, i: (0, 0)),
            pl.BlockSpec((HEAD_W, 1), lambda bi, h, i: (0, 0)),
        ],
        out_specs=pl.BlockSpec((None, tq, HEAD_W), lambda bi, h, i: (bi, i, h)),
        out_shape=jax.ShapeDtypeStruct((b, t, ATTN_W), BF16),
        scratch_shapes=[pltpu.VMEM((N_MAPS, HEAD_W, tq), F32), pltpu.VMEM((N_MAPS, tq, tq), F32),
                        pltpu.VMEM((N_MAPS, tq, tq), BF16)],
        compiler_params=_params(("arbitrary", "arbitrary", "arbitrary")),
    )(qt, k, vt, lp, sg_col)


def _decode_attn_body(pt_ref, q_ref, *refs, n_chunk, lam_init):
    del pt_ref
    k_refs = refs[:n_chunk]
    v_refs = refs[n_chunk:2 * n_chunk]
    (kn_ref, vn_ref, bias_ref, biasn_ref, lp_ref, sg_ref, o_ref,
     qm_ref, m_ref, l_ref, acc_ref, s_ref, sn_ref) = refs[2 * n_chunk:]
    j = pl.program_id(1)
    n_tok = ROWS_PER_HEAD // N_MAPS
    page_rows = PAGE_SIZE * N_HEADS

    @pl.when(j == 0)
    def _():
        q8 = q_ref[...]
        row = lax.broadcasted_iota(jnp.int32, (ROWS_PER_HEAD, HEAD_W), 0)
        lane = lax.broadcasted_iota(jnp.int32, (ROWS_PER_HEAD, HEAD_W), 1)
        keep = (lane // HEAD_DIM) == (row // n_tok)
        blocks = [jnp.where(keep, q8[:, h * HEAD_W:(h + 1) * HEAD_W], 0.0) for h in range(N_HEADS)]
        qm_ref[...] = jnp.concatenate(blocks, axis=0).astype(BF16)
        m_ref[...] = jnp.full(m_ref.shape, NEG_INF, F32)
        l_ref[...] = jnp.zeros(l_ref.shape, F32)
        acc_ref[...] = jnp.zeros(acc_ref.shape, F32)

    qm = qm_ref[...]

    def update(state, kmats, vmats, bias, stage_ref):
        m, l, acc = state
        m_new = m
        for c, kmat in enumerate(kmats):
            s = lax.dot_general(qm, kmat.astype(BF16), (((1,), (1,)), ((), ())),
                                preferred_element_type=F32) + bias
            stage_ref[c] = s
            m_new = jnp.maximum(m_new, jnp.max(s, axis=-1, keepdims=True))
        alpha = jnp.exp2(m - m_new)
        l = alpha * l
        acc = alpha * acc
        for c, vmat in enumerate(vmats):
            p = jnp.exp2(stage_ref[c] - m_new)
            l = l + jnp.sum(p, axis=-1, keepdims=True)
            acc = acc + jnp.dot(p.astype(BF16), vmat.astype(BF16), preferred_element_type=F32)
        return m_new, l, acc

    state = update((m_ref[...], l_ref[...], acc_ref[...]),
                   [k_refs[c][...].reshape(page_rows, HEAD_W) for c in range(n_chunk)],
                   [v_refs[c][...].reshape(page_rows, HEAD_W) for c in range(n_chunk)],
                   bias_ref[...], s_ref)
    m_ref[...], l_ref[...], acc_ref[...] = state

    @pl.when(j == pl.num_programs(1) - 1)
    def _():
        pad = jnp.zeros((PAGE_SIZE - kn_ref.shape[0], HEAD_W), F32)
        kn = jnp.concatenate([kn_ref[...], pad], axis=0)
        vn = jnp.concatenate([vn_ref[...], pad], axis=0)
        _, l, acc = update(state, [kn], [vn], biasn_ref[...], sn_ref)
        o = acc / l
        lam = _lam(lp_ref, lam_init)
        for h in range(N_HEADS):
            blk = o[h * ROWS_PER_HEAD:(h + 1) * ROWS_PER_HEAD]
            r = blk[:n_tok] - lam * blk[n_tok:]
            ms = jnp.mean(r * r, axis=-1, keepdims=True)
            o_ref[:, h * HEAD_W:(h + 1) * HEAD_W] = (
                r * lax.rsqrt(ms + NORM_EPS) * sg_ref[...] * (1.0 - lam_init))


def _decode_bias(n_tok):
    n = jnp.arange(N_HEADS * ROWS_PER_HEAD)[:, None]
    col = jnp.arange(PAGE_SIZE * N_HEADS)[None, :]
    same_head = (col % N_HEADS) == (n // ROWS_PER_HEAD)
    bias = jnp.where(same_head, 0.0, NEG_INF).astype(F32)
    coln = jnp.arange(PAGE_SIZE)[None, :]
    ok = (coln < n_tok * N_HEADS) & ((coln % N_HEADS) == (n // ROWS_PER_HEAD)) & \
         ((coln // N_HEADS) <= (n % n_tok))
    return bias, jnp.where(ok, 0.0, NEG_INF).astype(F32)


def _decode_attn(page_table, q8, cache_k, cache_v, kn, vn, lp, sg, lam_init, n_chunk):
    nb, n_pages = page_table.shape
    n_tok = ROWS_PER_HEAD // N_MAPS
    n_rows = N_HEADS * ROWS_PER_HEAD
    bias, biasn = _decode_bias(n_tok)

    def page_spec(c):
        return pl.BlockSpec((None, None, PAGE_SIZE, N_HEADS, HEAD_W),
                            lambda b, j, pt: (0, pt[b, j * n_chunk + c], 0, 0, 0))

    def const_spec(shape):
        return pl.BlockSpec(shape, lambda b, j, pt: (0,) * len(shape))

    new_spec = pl.BlockSpec((None, n_tok * N_HEADS, HEAD_W), lambda b, j, pt: (b, 0, 0))
    in_specs = [pl.BlockSpec((None, ROWS_PER_HEAD, ATTN_W), lambda b, j, pt: (b, 0, 0))]
    in_specs += [page_spec(c) for c in range(n_chunk)]
    in_specs += [page_spec(c) for c in range(n_chunk)]
    in_specs += [new_spec, new_spec, const_spec(bias.shape), const_spec(biasn.shape),
                 const_spec((4, HEAD_DIM)), const_spec((1, HEAD_W))]
    grid_spec = pltpu.PrefetchScalarGridSpec(
        num_scalar_prefetch=1, grid=(nb, n_pages // n_chunk), in_specs=in_specs,
        out_specs=pl.BlockSpec((None, n_tok, ATTN_W), lambda b, j, pt: (b, 0, 0)),
        scratch_shapes=[pltpu.VMEM((n_rows, HEAD_W), BF16), pltpu.VMEM((n_rows, 1), F32),
                        pltpu.VMEM((n_rows, 1), F32), pltpu.VMEM((n_rows, HEAD_W), F32),
                        pltpu.VMEM((n_chunk, n_rows, PAGE_SIZE * N_HEADS), F32),
                        pltpu.VMEM((1, n_rows, PAGE_SIZE), F32)],
    )
    body = functools.partial(_decode_attn_body, n_chunk=n_chunk, lam_init=lam_init)
    return pl.pallas_call(
        body, grid_spec=grid_spec, out_shape=jax.ShapeDtypeStruct((nb, n_tok, ATTN_W), F32),
        compiler_params=_params(("arbitrary", "arbitrary")),
    )(page_table, q8, *([cache_k] * n_chunk), *([cache_v] * n_chunk), kn, vn, bias, biasn, lp, sg)


def _pool_mix(u, window_sum, cnt_of, wg_ref, ps_ref):
    outs = []
    for g, w in enumerate(POOL_WINDOWS):
        cols = slice(g * POOL_GROUP, (g + 1) * POOL_GROUP)
        d = window_sum(g, w, cols) * (1.0 / cnt_of(w)) - u[:, cols]
        y = jnp.dot(d.astype(BF16), wg_ref[g], preferred_element_type=F32)
        outs.append(y * ps_ref[:, cols])
    return jnp.concatenate(outs, axis=1)


def _sigmoid(x):
    return 0.5 * jnp.tanh(0.5 * x) + 0.5


def _merge_tail(x, oa, za, op, zp, ga, gp, wo_ref):
    ha = (oa * (za * _sigmoid(za))).astype(BF16)
    hp = (op * (zp * _sigmoid(zp))).astype(BF16)
    a = jnp.dot(ha, wo_ref[0:ATTN_W, :], preferred_element_type=F32)
    p = jnp.dot(hp, wo_ref[ATTN_W:ATTN_W + POOL_W, :], preferred_element_type=F32)
    return x + _sigmoid(ga) * a + _sigmoid(gp) * p


def _merge_prompt_body(x_ref, oa_ref, za_ref, u_ref, halo_ref, zp_ref, ga_ref, gp_ref,
                       wg_ref, ps_ref, wo_ref, y_ref, ext_ref, lvl_ref, *, tm):
    i = pl.program_id(1)
    u = u_ref[...]
    ext_ref[0:HALO, :] = jnp.where(i == 0, 0.0, halo_ref[...])
    ext_ref[HALO:HALO + tm, :] = u
    pos = i * tm + lax.broadcasted_iota(jnp.int32, (tm, 1), 0)

    def window_sum(g, w, cols):
        src, src_cols = ext_ref, cols
        for lvl in range(g + 1):
            shift = 1 << lvl
            start = HALO if lvl == g else SUBLANES * (lvl + 1)
            n = HALO + tm - start
            val = src[start:start + n, src_cols] + src[start - shift:start - shift + n, src_cols]
            if lvl == g:
                return val
            lvl_ref[lvl, start:start + n, :] = val
            src, src_cols = lvl_ref.at[lvl], slice(None)

    def cnt_of(w):
        return jnp.minimum(w, pos + 1).astype(F32)

    op = _pool_mix(u, window_sum, cnt_of, wg_ref, ps_ref)
    y_ref[...] = _merge_tail(x_ref[...], oa_ref[...].astype(F32), za_ref[...].astype(F32), op,
                             zp_ref[...].astype(F32), ga_ref[...].astype(F32), gp_ref[...].astype(F32), wo_ref)


def _merge_prompt(x, oa, gates, u, wg, ps, wo, tm):
    b, t, _ = x.shape

    def gate_spec(c):
        return pl.BlockSpec((None, None, tm, D_MODEL), lambda bi, i: (c, bi, i, 0))

    row_spec = pl.BlockSpec((None, tm, D_MODEL), lambda bi, i: (bi, i, 0))
    halo_spec = pl.BlockSpec((None, HALO, D_MODEL),
                             lambda bi, i: (bi, jnp.maximum(i * (tm // HALO) - 1, 0), 0))
    body = functools.partial(_merge_prompt_body, tm=tm)
    return pl.pallas_call(
        body, grid=(b, t // tm),
        in_specs=[row_spec, row_spec, gate_spec(0), row_spec, halo_spec, gate_spec(1),
                  gate_spec(2), gate_spec(3),
                  pl.BlockSpec((len(POOL_WINDOWS), POOL_GROUP, POOL_GROUP), lambda bi, i: (0, 0, 0)),
                  pl.BlockSpec((1, POOL_W), lambda bi, i: (0, 0)),
                  pl.BlockSpec((ATTN_W + POOL_W, D_MODEL), lambda bi, i: (0, 0))],
        out_specs=row_spec,
        out_shape=jax.ShapeDtypeStruct((b, t, D_MODEL), F32),
        scratch_shapes=[pltpu.VMEM((HALO + tm, D_MODEL), F32),
                        pltpu.VMEM((len(POOL_WINDOWS) - 1, HALO + tm, POOL_GROUP), F32)],
        compiler_params=_params(("arbitrary", "arbitrary")),
    )(x, oa, gates, u, u, gates, gates, gates, wg, ps, wo)


def _merge_sample_body(x_ref, oa_ref, za_ref, zp_ref, ga_ref, gp_ref, ext_ref,
                       wg_ref, ps_ref, wo_ref, y_ref, *, start_pos):
    t = pl.program_id(0)
    u = ext_ref[POOL_HIST + t]

    def window_sum(g, w, cols):
        del g
        s = u[:, cols]
        for k in range(1, w):
            s = s + ext_ref[POOL_HIST + t - k][:, cols]
        return s

    def cnt_of(w):
        return jnp.minimum(w, start_pos + t + 1).astype(F32)

    op = _pool_mix(u, window_sum, cnt_of, wg_ref, ps_ref)
    y_ref[...] = _merge_tail(x_ref[...], oa_ref[...].astype(F32), za_ref[...].astype(F32), op,
                             zp_ref[...].astype(F32), ga_ref[...].astype(F32), gp_ref[...].astype(F32), wo_ref)


def _merge_sample(x, oa, gates, ext, wg, ps, wo, start_pos):
    t, b, _ = x.shape

    def gate_spec(c):
        return pl.BlockSpec((None, None, b, D_MODEL), lambda ti: (c, ti, 0, 0))

    row_spec = pl.BlockSpec((None, b, D_MODEL), lambda ti: (ti, 0, 0))
    body = functools.partial(_merge_sample_body, start_pos=start_pos)
    return pl.pallas_call(
        body, grid=(t,),
        in_specs=[row_spec, row_spec, gate_spec(0), gate_spec(1), gate_spec(2), gate_spec(3),
                  pl.BlockSpec(ext.shape, lambda ti: (0, 0, 0)),
                  pl.BlockSpec((len(POOL_WINDOWS), POOL_GROUP, POOL_GROUP), lambda ti: (0, 0, 0)),
                  pl.BlockSpec((1, POOL_W), lambda ti: (0, 0)),
                  pl.BlockSpec((ATTN_W + POOL_W, D_MODEL), lambda ti: (0, 0))],
        out_specs=row_spec,
        out_shape=jax.ShapeDtypeStruct((t, b, D_MODEL), F32),
        compiler_params=_params(("arbitrary",)),
    )(x, oa, gates, gates, gates, gates, ext, wg, ps, wo)


Q_SCALE = ATTN_SCALE * math.log2(math.e)
PROJ_ROWS = 256
PROMPT_ATTN_TILE = 512
PROMPT_MERGE_ROWS = 512
DECODE_PAGES_PER_STEP = 16


def kernel(x_prompt, x_sample, cache_k, cache_v, state_pool, page_table, g_norm, w_in, q_norm_g,
           k_norm_g, lambda_params, subln_g, w_grp, pool_scale, w_out):
    depth = g_norm.shape[0]
    assert depth == 1
    l = 0
    lam_init = 0.8 - 0.6 * math.exp(-0.3 * l)
    bp, tp, _ = x_prompt.shape
    bs, ts, _ = x_sample.shape
    assert ts * N_MAPS == ROWS_PER_HEAD
    past_len = page_table.shape[1] * PAGE_SIZE

    w_bf16 = w_in[l].astype(BF16)
    wo = w_out[l].astype(BF16)
    wg = w_grp[l].astype(BF16)
    gn = g_norm[l].reshape(1, D_MODEL)
    gq = jnp.tile(q_norm_g[l], N_HEADS * N_MAPS).reshape(1, ATTN_W)
    gk = jnp.tile(k_norm_g[l], N_HEADS * N_MAPS).reshape(1, ATTN_W)
    grp = jnp.arange(ATTN_W) // HEAD_DIM
    gmat = jnp.where(grp[:, None] == grp[None, :], 1.0 / HEAD_DIM, 0.0).astype(BF16)
    lp = lambda_params[l]
    sg = subln_g[l].reshape(1, HEAD_W)
    ps = pool_scale[l].reshape(1, POOL_W)

    xp2d = x_prompt.reshape(bp * tp, D_MODEL)
    q, k32, k16, v32, v16, u, gates = _in_proj(xp2d, gn, w_bf16, gq, gk, gmat, PROJ_ROWS, Q_SCALE, BF16)
    tq = PROMPT_ATTN_TILE
    qt = jnp.swapaxes(q.reshape(bp, tp, ATTN_W), 1, 2)
    vt = v16.reshape(bp, tp // tq, tq, N_HEADS, HEAD_W).transpose(0, 3, 1, 4, 2)
    oa = _prompt_attn(qt, k16.reshape(bp, tp, ATTN_W), vt, lp, sg.reshape(HEAD_W, 1), lam_init, tq)
    u = u.reshape(bp, tp, D_MODEL)
    gates = gates.reshape(len(GATE_GROUPS), bp, tp, D_MODEL)
    y_prompt = _merge_prompt(x_prompt, oa, gates, u, wg, ps, wo, PROMPT_MERGE_ROWS)
    k_prompt = k32.reshape(1, bp, tp, N_HEADS, HEAD_W)
    v_prompt = v32.reshape(1, bp, tp, N_HEADS, HEAD_W)
    pool_prompt = u[:, tp - POOL_HIST:][None]

    xs_tm = jnp.swapaxes(x_sample, 0, 1)
    qs, ks32, _, vs32, _, u_tm, gates_s = _in_proj(xs_tm.reshape(ts * bs, D_MODEL), gn, w_bf16, gq, gk, gmat,
                                            PROJ_ROWS, Q_SCALE, F32)

    def seq_major(a):
        return jnp.swapaxes(a.reshape(ts, bs, ATTN_W), 0, 1)

    qs, ks, vs = seq_major(qs), seq_major(ks32), seq_major(vs32)
    q8 = jnp.concatenate([qs, qs], axis=1)
    kn = ks.reshape(bs, ts * N_HEADS, HEAD_W)
    vn = vs.reshape(bs, ts * N_HEADS, HEAD_W)
    oa_s = _decode_attn(page_table, q8, cache_k, cache_v, kn, vn, lp, sg, lam_init, DECODE_PAGES_PER_STEP)
    u_tm = u_tm.reshape(ts, bs, D_MODEL)
    gates_s = gates_s.reshape(len(GATE_GROUPS), ts, bs, D_MODEL)
    ext = jnp.concatenate([jnp.swapaxes(state_pool[l], 0, 1), u_tm], axis=0)
    y_s = _merge_sample(xs_tm, jnp.swapaxes(oa_s, 0, 1), gates_s, ext, wg, ps, wo, past_len)
    y_sample = jnp.swapaxes(y_s, 0, 1)
    k_sample = ks.reshape(1, bs, ts, N_HEADS, HEAD_W)
    v_sample = vs.reshape(1, bs, ts, N_HEADS, HEAD_W)
    u_s = jnp.swapaxes(u_tm, 0, 1)
    pool_sample = jnp.concatenate([state_pool[l][:, ts:], u_s], axis=1)[None]

    return (y_prompt, y_sample, k_prompt, v_prompt, pool_prompt, k_sample, v_sample, pool_sample)
```
